```python
import math
import jax, jax.numpy as jnp
from jax import lax
import numpy as np

D_MODEL = 2048
BATCH = 4
SEQ = 2048
DEPTH = 4
DEC_BATCH = 32
DEC_SEQ = 1
PAST_LEN = 16384
PAGE_SIZE = 128

MIX_WIDTH = D_MODEL
HEAD_DIM = 64
ATTN_WIDTH = MIX_WIDTH // 2
N_HEADS = ATTN_WIDTH // HEAD_DIM
N_KV_HEADS = max(1, N_HEADS // 8)
GRP = N_HEADS // N_KV_HEADS
WINDOW = 128
SSM_WIDTH = MIX_WIDTH - ATTN_WIDTH
SSM_CH_GROUP = 16
SSM_GROUPS = SSM_WIDTH // SSM_CH_GROUP
SSM_STATE = 64
D_FF = -(-8 * D_MODEL // (3 * 256)) * 256
PLE_DIM = 256
Q_WIDTH = N_HEADS * HEAD_DIM
KV_WIDTH = N_KV_HEADS * HEAD_DIM
IN_WIDTH = Q_WIDTH + 2 * KV_WIDTH + SSM_WIDTH
NORM_EPS = 1e-6
NEG_INF = -1e30

kernel_name = 'hymba_swa_sink_s5_decoder_step'


def rms_norm(x, g):
    xf = x.astype(jnp.float32)
    var = jnp.mean(xf * xf, axis=-1, keepdims=True)
    return (xf * lax.rsqrt(var + NORM_EPS) * g.astype(jnp.float32)).astype(x.dtype)


def alibi_slopes():
    return 2.0 ** (-8.0 * jnp.arange(1, N_HEADS + 1, dtype=jnp.float32) / N_HEADS)


def sink_attend(q, keys, vals, dist, valid, sinks):
    s = jnp.einsum('...qkgd,...skd->...kgqs', q.astype(jnp.float32), keys.astype(jnp.float32)) * (HEAD_DIM ** -0.5)
    slopes = alibi_slopes().reshape(N_KV_HEADS, GRP, 1, 1)
    s = jnp.where(valid, s - slopes * dist, NEG_INF)
    sink = sinks.astype(jnp.float32).reshape(N_KV_HEADS, GRP, 1, 1)
    m = jnp.maximum(jnp.max(s, axis=-1, keepdims=True), sink)
    e = jnp.exp(s - m)
    p = e / (jnp.sum(e, axis=-1, keepdims=True) + jnp.exp(sink - m))
    out = jnp.einsum('...kgqs,...skd->...qkgd', p, vals.astype(jnp.float32))
    return out.astype(q.dtype)


def attn_prompt(q, k, v, sinks):
    n, t = q.shape[:2]
    nb = t // WINDOW
    qb = q.reshape(n, nb, WINDOW, N_KV_HEADS, GRP, HEAD_DIM)
    kb = k.reshape(n, nb, WINDOW, N_KV_HEADS, HEAD_DIM)
    vb = v.reshape(n, nb, WINDOW, N_KV_HEADS, HEAD_DIM)
    pad = ((0, 0), (1, 0), (0, 0), (0, 0), (0, 0))
    keys = jnp.concatenate([jnp.pad(kb, pad)[:, :-1], kb], axis=2)
    vals = jnp.concatenate([jnp.pad(vb, pad)[:, :-1], vb], axis=2)
    i = jnp.arange(WINDOW)[:, None]
    j = jnp.arange(2 * WINDOW)[None, :]
    dist = i - j + WINDOW
    blk = jnp.arange(nb)[:, None, None]
    valid = (dist >= 0) & (dist <= WINDOW) & ((blk > 0) | (j >= WINDOW))
    out = sink_attend(qb, keys, vals, dist.astype(jnp.float32), valid[:, None, None], sinks)
    return out.reshape(n, t, Q_WIDTH), k[:, -WINDOW:], v[:, -WINDOW:]


def attn_sample(q, k, v, sinks, k_buf, v_buf):
    n, t = q.shape[:2]
    keys = jnp.concatenate([k_buf.astype(k.dtype), k], axis=1)
    vals = jnp.concatenate([v_buf.astype(v.dtype), v], axis=1)
    i = jnp.arange(t)[:, None]
    j = jnp.arange(WINDOW + t)[None, :]
    dist = i - j + WINDOW
    valid = (dist >= 0) & (dist <= WINDOW)
    out = sink_attend(q, keys, vals, dist.astype(jnp.float32), valid, sinks)
    return out.reshape(n, t, Q_WIDTH), keys[:, -WINDOW:], vals[:, -WINDOW:]


def _cplx_combine(x, y):
    ar1, ai1, br1, bi1 = x
    ar2, ai2, br2, bi2 = y
    return (ar2 * ar1 - ai2 * ai1, ar2 * ai1 + ai2 * ar1,
            ar2 * br1 - ai2 * bi1 + br2, ar2 * bi1 + ai2 * br1 + bi2)


def ssm_mix(u, prm, s0):
    n, t = u.shape[:2]
    f32 = jnp.float32
    uf = u.astype(f32).reshape(n, t, SSM_GROUPS, SSM_CH_GROUP)
    a_re = prm['ssm_a_re'].astype(f32)
    a_im = prm['ssm_a_im'].astype(f32)
    dt = jnp.exp(prm['ssm_log_dt'].astype(f32))[:, None]
    dta_re, dta_im = dt * a_re, dt * a_im
    mag = jnp.exp(dta_re)
    ab_re, ab_im = mag * jnp.cos(dta_im), mag * jnp.sin(dta_im)
    den = a_re * a_re + a_im * a_im
    f_re = ((ab_re - 1.0) * a_re + ab_im * a_im) / den
    f_im = (ab_im * a_re - (ab_re - 1.0) * a_im) / den
    b_re = prm['ssm_b_re'].astype(f32)
    b_im = prm['ssm_b_im'].astype(f32)
    bb_re = f_re[..., None] * b_re - f_im[..., None] * b_im
    bb_im = f_re[..., None] * b_im + f_im[..., None] * b_re
    bu_re = jnp.einsum('ntgc,gpc->ntgp', uf, bb_re)
    bu_im = jnp.einsum('ntgc,gpc->ntgp', uf, bb_im)
    a_seq_re = jnp.broadcast_to(ab_re, (1, t) + ab_re.shape)
    a_seq_im = jnp.broadcast_to(ab_im, (1, t) + ab_im.shape)
    _, _, s_re, s_im = lax.associative_scan(_cplx_combine, (a_seq_re, a_seq_im, bu_re, bu_im), axis=1)
    if s0 is not None:
        kk = jnp.arange(1, t + 1, dtype=f32)[:, None, None]
        pmag = jnp.exp(kk * dta_re)
        pr, pi = pmag * jnp.cos(kk * dta_im), pmag * jnp.sin(kk * dta_im)
        s0r = s0[0].astype(f32)[:, None]
        s0i = s0[1].astype(f32)[:, None]
        s_re, s_im = s_re + pr * s0r - pi * s0i, s_im + pr * s0i + pi * s0r
    y = (jnp.einsum('ntgp,gcp->ntgc', s_re, prm['ssm_c_re'].astype(f32))
         - jnp.einsum('ntgp,gcp->ntgc', s_im, prm['ssm_c_im'].astype(f32))
         + prm['ssm_d'].astype(f32) * uf)
    z = jnp.einsum('ntgc,gce->ntge', jax.nn.gelu(y), prm['ssm_w_glu'].astype(f32))
    out = z[..., :SSM_CH_GROUP] * jax.nn.sigmoid(z[..., SSM_CH_GROUP:])
    return out.reshape(n, t, SSM_WIDTH).astype(u.dtype), s_re[:, -1], s_im[:, -1]


def trunk_layer(x, pe, prm, cache):
    n, t, _ = x.shape
    h = rms_norm(x, prm['g_pre_mix'])
    z = h @ prm['w_in']
    q = z[..., :Q_WIDTH].reshape(n, t, N_KV_HEADS, GRP, HEAD_DIM)
    k = z[..., Q_WIDTH:Q_WIDTH + KV_WIDTH].reshape(n, t, N_KV_HEADS, HEAD_DIM)
    v = z[..., Q_WIDTH + KV_WIDTH:Q_WIDTH + 2 * KV_WIDTH].reshape(n, t, N_KV_HEADS, HEAD_DIM)
    u = z[..., Q_WIDTH + 2 * KV_WIDTH:]
    if cache is None:
        attn, k_new, v_new = attn_prompt(q, k, v, prm['attn_sinks'])
        s0 = None
    else:
        k_buf, v_buf, s_re0, s_im0 = cache
        attn, k_new, v_new = attn_sample(q, k, v, prm['attn_sinks'], k_buf, v_buf)
        s0 = (s_re0, s_im0)
    ssm, s_re, s_im = ssm_mix(u, prm, s0)
    merged = jnp.concatenate([rms_norm(attn, prm['g_attn_out']), rms_norm(ssm, prm['g_ssm_out'])], axis=-1)
    x = x + rms_norm(merged @ prm['w_out'], prm['g_post_mix'])
    h = rms_norm(x, prm['g_pre_ffn'])
    gu = h @ prm['w_gate_up']
    f = (jax.nn.silu(gu[..., :D_FF]) * gu[..., D_FF:]) @ prm['w_down']
    x = x + rms_norm(f, prm['g_post_ffn'])
    x = x + jax.nn.sigmoid(x @ prm['w_ple_gate']) * (pe @ prm['w_ple_proj'])
    return x, k_new, v_new, s_re, s_im


def setup_inputs(seed: int = 0) -> dict:
    key = jax.random.key(seed)
    ks = jax.random.split(key, 40)
    f32 = jnp.float32

    def nrm(k, shape, scale=1.0):
        return scale * jax.random.normal(k, shape, f32)

    def gain(k, shape):
        return 1.0 + 0.05 * jax.random.normal(k, shape, f32)

    n_idx = jnp.arange(SSM_STATE, dtype=f32)
    return {
        'x_prompt': nrm(ks[0], (BATCH, SEQ, D_MODEL)),
        'x_sample': nrm(ks[1], (DEC_BATCH, DEC_SEQ, D_MODEL)),
        'cache_k': nrm(ks[2], (DEPTH, DEC_BATCH, WINDOW, N_KV_HEADS, HEAD_DIM)),
        'cache_v': nrm(ks[3], (DEPTH, DEC_BATCH, WINDOW, N_KV_HEADS, HEAD_DIM)),
        'state_ssm_re': nrm(ks[4], (DEPTH, DEC_BATCH, SSM_GROUPS, SSM_STATE), 0.5),
        'state_ssm_im': nrm(ks[5], (DEPTH, DEC_BATCH, SSM_GROUPS, SSM_STATE), 0.5),
        'p_prompt': nrm(ks[6], (DEPTH, BATCH, SEQ, PLE_DIM)),
        'p_sample': nrm(ks[7], (DEPTH, DEC_BATCH, DEC_SEQ, PLE_DIM)),
        'g_pre_mix': gain(ks[8], (DEPTH, D_MODEL)),
        'w_in': nrm(ks[9], (DEPTH, D_MODEL, IN_WIDTH), D_MODEL ** -0.5),
        'attn_sinks': nrm(ks[10], (DEPTH, N_HEADS), 0.5),
        'ssm_a_re': -0.5 + nrm(ks[11], (DEPTH, SSM_GROUPS, SSM_STATE), 0.01),
        'ssm_a_im': math.pi * n_idx + nrm(ks[12], (DEPTH, SSM_GROUPS, SSM_STATE), 0.01),
        'ssm_log_dt': jax.random.uniform(ks[13], (DEPTH, SSM_GROUPS), f32, minval=math.log(0.001), maxval=math.log(0.1)),
        'ssm_b_re': nrm(ks[14], (DEPTH, SSM_GROUPS, SSM_STATE, SSM_CH_GROUP), (2 * SSM_CH_GROUP) ** -0.5),
        'ssm_b_im': nrm(ks[15], (DEPTH, SSM_GROUPS, SSM_STATE, SSM_CH_GROUP), (2 * SSM_CH_GROUP) ** -0.5),
        'ssm_c_re': nrm(ks[16], (DEPTH, SSM_GROUPS, SSM_CH_GROUP, SSM_STATE), (2 * SSM_STATE) ** -0.5),
        'ssm_c_im': nrm(ks[17], (DEPTH, SSM_GROUPS, SSM_CH_GROUP, SSM_STATE), (2 * SSM_STATE) ** -0.5),
        'ssm_d': nrm(ks[18], (DEPTH, SSM_GROUPS, SSM_CH_GROUP)),
        'ssm_w_glu': nrm(ks[19], (DEPTH, SSM_GROUPS, SSM_CH_GROUP, 2 * SSM_CH_GROUP), SSM_CH_GROUP ** -0.5),
        'g_attn_out': gain(ks[20], (DEPTH, ATTN_WIDTH)),
        'g_ssm_out': gain(ks[21], (DEPTH, SSM_WIDTH)),
        'w_out': nrm(ks[22], (DEPTH, MIX_WIDTH, D_MODEL), MIX_WIDTH ** -0.5),
        'g_post_mix': gain(ks[23], (DEPTH, D_MODEL)),
        'g_pre_ffn': gain(ks[24], (DEPTH, D_MODEL)),
        'w_gate_up': nrm(ks[25], (DEPTH, D_MODEL, 2 * D_FF), D_MODEL ** -0.5),
        'w_down': nrm(ks[26], (DEPTH, D_FF, D_MODEL), D_FF ** -0.5),
        'g_post_ffn': gain(ks[27], (DEPTH, D_MODEL)),
        'w_ple_gate': nrm(ks[28], (DEPTH, D_MODEL, D_MODEL), D_MODEL ** -0.5),
        'w_ple_proj': nrm(ks[29], (DEPTH, PLE_DIM, D_MODEL), PLE_DIM ** -0.5),
    }


def reference(x_prompt, x_sample, cache_k, cache_v, state_ssm_re, state_ssm_im, p_prompt, p_sample,
              g_pre_mix, w_in, attn_sinks, ssm_a_re, ssm_a_im, ssm_log_dt, ssm_b_re, ssm_b_im,
              ssm_c_re, ssm_c_im, ssm_d, ssm_w_glu, g_attn_out, g_ssm_out, w_out, g_post_mix,
              g_pre_ffn, w_gate_up, w_down, g_post_ffn, w_ple_gate, w_ple_proj):
    xp, xs = x_prompt, x_sample
    kp_l, vp_l, srp_l, sip_l = [], [], [], []
    ks_l, vs_l, srs_l, sis_l = [], [], [], []
    for l in range(DEPTH):
        prm = {
            'g_pre_mix': g_pre_mix[l], 'w_in': w_in[l], 'attn_sinks': attn_sinks[l],
            'ssm_a_re': ssm_a_re[l], 'ssm_a_im': ssm_a_im[l], 'ssm_log_dt': ssm_log_dt[l],
            'ssm_b_re': ssm_b_re[l], 'ssm_b_im': ssm_b_im[l], 'ssm_c_re': ssm_c_re[l],
            'ssm_c_im': ssm_c_im[l], 'ssm_d': ssm_d[l], 'ssm_w_glu': ssm_w_glu[l],
            'g_attn_out': g_attn_out[l], 'g_ssm_out': g_ssm_out[l], 'w_out': w_out[l],
            'g_post_mix': g_post_mix[l], 'g_pre_ffn': g_pre_ffn[l], 'w_gate_up': w_gate_up[l],
            'w_down': w_down[l], 'g_post_ffn': g_post_ffn[l], 'w_ple_gate': w_ple_gate[l],
            'w_ple_proj': w_ple_proj[l],
        }
        xp, kp, vp, srp, sip = trunk_layer(xp, p_prompt[l], prm, None)
        xs, kk, vv, srs, sis = trunk_layer(
            xs, p_sample[l], prm, (cache_k[l], cache_v[l], state_ssm_re[l], state_ssm_im[l]))
        kp_l.append(kp); vp_l.append(vp); srp_l.append(srp); sip_l.append(sip)
        ks_l.append(kk); vs_l.append(vv); srs_l.append(srs); sis_l.append(sis)
    return (xp, xs,
            jnp.stack(kp_l), jnp.stack(vp_l), jnp.stack(srp_l), jnp.stack(sip_l),
            jnp.stack(ks_l), jnp.stack(vs_l), jnp.stack(srs_l), jnp.stack(sis_l))
```

```python
import functools
import math

import numpy as np
import jax
import jax.numpy as jnp
from jax import lax
from jax.experimental import pallas as pl
from jax.experimental.pallas import tpu as pltpu

F32 = jnp.float32
BF16 = jnp.bfloat16

D_MODEL = 2048
DEPTH = 4
HEAD_DIM = 64
N_HEADS = 16
N_KV = 2
GRP = N_HEADS // N_KV
WINDOW = 128
Q_WIDTH = N_HEADS * HEAD_DIM
KV_WIDTH = N_KV * HEAD_DIM
SSM_WIDTH = 1024
SSM_CG = 16
SSM_GROUPS = 64
SSM_STATE = 64
IN_WIDTH = Q_WIDTH + 2 * KV_WIDTH + SSM_WIDTH
D_FF = 5632
PLE_DIM = 256
EPS = 1e-6
NEG_INF = -1e30

SLAB_GROUPS = 16
N_SLABS = SSM_GROUPS // SLAB_GROUPS
SLAB_CH = SLAB_GROUPS * SSM_CG
SLAB_ST = SLAB_GROUPS * SSM_STATE
LANES = 128
SUBLANES = 8
ST_TILES = SLAB_ST // LANES
SSM_CHUNK = 256
SSM_PITCH = SSM_CHUNK + SUBLANES

VMEM_LIMIT = 56 * 1024 * 1024

_SLOPES = [float(v) for v in
           (2.0 ** (-8.0 * np.arange(1, N_HEADS + 1, dtype=np.float32) / N_HEADS)).astype(np.float32)]


def _rms(x, g):
    var = jnp.mean(x * x, axis=-1, keepdims=True)
    return x * lax.rsqrt(var + EPS) * g


def _gelu_tanh(x):
    c = math.sqrt(2.0 / math.pi)
    return 0.5 * x * (1.0 + jnp.tanh(c * (x + 0.044715 * (x * x * x))))


def _params(sem):
    return pltpu.CompilerParams(dimension_semantics=sem, vmem_limit_bytes=VMEM_LIMIT)


def _inproj_kernel(x_ref, g_ref, w_ref, o_ref, h_ref):
    @pl.when(pl.program_id(1) == 0)
    def _():
        h_ref[...] = _rms(x_ref[...], g_ref[...]).astype(BF16)

    o_ref[...] = jnp.dot(h_ref[...], w_ref[...].astype(BF16), preferred_element_type=F32)


def _inproj(x, g, w, tm, tn=256):
    m = x.shape[0]
    n = w.shape[1]
    return pl.pallas_call(
        _inproj_kernel,
        grid=(m // tm, n // tn),
        in_specs=[
            pl.BlockSpec((tm, D_MODEL), lambda i, j: (i, 0)),
            pl.BlockSpec((1, D_MODEL), lambda i, j: (0, 0)),
            pl.BlockSpec((D_MODEL, tn), lambda i, j: (0, j)),
        ],
        out_specs=pl.BlockSpec((tm, tn), lambda i, j: (i, j)),
        out_shape=jax.ShapeDtypeStruct((m, n), F32),
        scratch_shapes=[pltpu.VMEM((tm, D_MODEL), BF16)],
        compiler_params=_params(("parallel", "arbitrary")),
        name="inproj",
    )(x, g, w)


def _attn_prompt_kernel(sink_ref, q_ref, kp_ref, kc_ref, vp_ref, vc_ref, o_ref):
    blk = pl.program_id(1)
    kcat = jnp.concatenate([kp_ref[0], kc_ref[0]], axis=0).astype(BF16)
    vcat = jnp.concatenate([vp_ref[0], vc_ref[0]], axis=0).astype(BF16)
    qi = lax.broadcasted_iota(jnp.int32, (WINDOW, 2 * WINDOW), 0)
    kj = lax.broadcasted_iota(jnp.int32, (WINDOW, 2 * WINDOW), 1)
    dist_i = qi - kj + WINDOW
    valid = (dist_i >= 0) & (dist_i <= WINDOW) & ((blk > 0) | (kj >= WINDOW))
    dist = dist_i.astype(F32)
    for h in range(N_HEADS):
        kv = h // GRP
        qh = q_ref[0, :, HEAD_DIM * h:HEAD_DIM * (h + 1)].astype(BF16)
        kh = kcat[:, HEAD_DIM * kv:HEAD_DIM * (kv + 1)]
        vh = vcat[:, HEAD_DIM * kv:HEAD_DIM * (kv + 1)]
        s = lax.dot_general(qh, kh, (((1,), (1,)), ((), ())), preferred_element_type=F32)
        s = s * (HEAD_DIM ** -0.5)
        s = jnp.where(valid, s - _SLOPES[h] * dist, NEG_INF)
        sink = sink_ref[0, h]
        m = jnp.maximum(jnp.max(s, axis=-1, keepdims=True), sink)
        e = jnp.exp(s - m)
        denom = jnp.sum(e, axis=-1, keepdims=True) + jnp.exp(sink - m)
        oh = jnp.dot(e.astype(BF16), vh, preferred_element_type=F32) / denom
        o_ref[0, :, HEAD_DIM * h:HEAD_DIM * (h + 1)] = oh


def _attn_prompt(z3, sinks):
    nb, t, _ = z3.shape
    qblk = Q_WIDTH // KV_WIDTH
    return pl.pallas_call(
        _attn_prompt_kernel,
        grid=(nb, t // WINDOW),
        in_specs=[
            pl.BlockSpec(memory_space=pltpu.SMEM),
            pl.BlockSpec((1, WINDOW, Q_WIDTH), lambda b, i: (b, i, 0)),
            pl.BlockSpec((1, WINDOW, KV_WIDTH), lambda b, i: (b, jnp.maximum(i - 1, 0), qblk)),
            pl.BlockSpec((1, WINDOW, KV_WIDTH), lambda b, i: (b, i, qblk)),
            pl.BlockSpec((1, WINDOW, KV_WIDTH), lambda b, i: (b, jnp.maximum(i - 1, 0), qblk + 1)),
            pl.BlockSpec((1, WINDOW, KV_WIDTH), lambda b, i: (b, i, qblk + 1)),
        ],
        out_specs=pl.BlockSpec((1, WINDOW, Q_WIDTH), lambda b, i: (b, i, 0)),
        out_shape=jax.ShapeDtypeStruct((nb, t, Q_WIDTH), F32),
        compiler_params=_params(("parallel", "arbitrary")),
        name="attn_prompt",
    )(sinks, z3, z3, z3, z3, z3)


def _attn_sample_kernel(q_ref, kn_ref, vn_ref, ck_ref, cv_ref, sink_ref, slope_ref, o_ref):
    ck = ck_ref[0].astype(BF16)
    cv = cv_ref[0].astype(BF16)
    kn = kn_ref[0]
    vn = vn_ref[0]
    dist = (WINDOW - lax.broadcasted_iota(jnp.int32, (GRP, WINDOW), 1)).astype(F32)
    for kv in range(N_KV):
        lo, hi = HEAD_DIM * kv, HEAD_DIM * (kv + 1)
        qk = q_ref[0, GRP * kv:GRP * (kv + 1), :]
        slope = slope_ref[GRP * kv:GRP * (kv + 1), :]
        sink = sink_ref[GRP * kv:GRP * (kv + 1), :]
        s_c = lax.dot_general(qk.astype(BF16), ck[:, lo:hi], (((1,), (1,)), ((), ())),
                              preferred_element_type=F32) * (HEAD_DIM ** -0.5)
        s_c = s_c - slope * dist
        s_n = jnp.sum(qk * kn[:, lo:hi], axis=-1, keepdims=True) * (HEAD_DIM ** -0.5)
        m = jnp.maximum(jnp.maximum(jnp.max(s_c, axis=-1, keepdims=True), s_n), sink)
        e_c = jnp.exp(s_c - m)
        e_n = jnp.exp(s_n - m)
        denom = jnp.sum(e_c, axis=-1, keepdims=True) + e_n + jnp.exp(sink - m)
        o = jnp.dot(e_c.astype(BF16), cv[:, lo:hi], preferred_element_type=F32) + e_n * vn[:, lo:hi]
        o_ref[0, GRP * kv:GRP * (kv + 1), :] = o / denom


def _attn_sample(q3, kn, vn, ck, cv, sink_col, slope_col):
    n = q3.shape[0]
    return pl.pallas_call(
        _attn_sample_kernel,
        grid=(n,),
        in_specs=[
            pl.BlockSpec((1, N_HEADS, HEAD_DIM), lambda i: (i, 0, 0)),
            pl.BlockSpec((1, 1, KV_WIDTH), lambda i: (i, 0, 0)),
            pl.BlockSpec((1, 1, KV_WIDTH), lambda i: (i, 0, 0)),
            pl.BlockSpec((1, WINDOW, KV_WIDTH), lambda i: (i, 0, 0)),
            pl.BlockSpec((1, WINDOW, KV_WIDTH), lambda i: (i, 0, 0)),
            pl.BlockSpec((N_HEADS, 1), lambda i: (0, 0)),
            pl.BlockSpec((N_HEADS, 1), lambda i: (0, 0)),
        ],
        out_specs=pl.BlockSpec((1, N_HEADS, HEAD_DIM), lambda i: (i, 0, 0)),
        out_shape=jax.ShapeDtypeStruct((n, N_HEADS, HEAD_DIM), F32),
        compiler_params=_params(("parallel",)),
        name="attn_sample",
    )(q3, kn, vn, ck, cv, sink_col, slope_col)


def _ssm_prep_kernel(are_ref, aim_ref, ldt_ref, bre_ref, bim_ref,
                     abre_ref, abim_ref, bbre_ref, bbim_ref):
    a_re = are_ref[...]
    a_im = aim_ref[...]
    dt = jnp.exp(ldt_ref[...])
    dta_re = dt * a_re
    dta_im = dt * a_im
    mag = jnp.exp(dta_re)
    ab_re = mag * jnp.cos(dta_im)
    ab_im = mag * jnp.sin(dta_im)
    den = a_re * a_re + a_im * a_im
    f_re = ((ab_re - 1.0) * a_re + ab_im * a_im) / den
    f_im = (ab_im * a_re - (ab_re - 1.0) * a_im) / den
    abre_ref[...] = ab_re
    abim_ref[...] = ab_im
    b_re = bre_ref[...]
    b_im = bim_ref[...]
    bbre_ref[...] = f_re[:, None, :] * b_re - f_im[:, None, :] * b_im
    bbim_ref[...] = f_re[:, None, :] * b_im + f_im[:, None, :] * b_re


def _ssm_prep(a_re, a_im, log_dt, b_re_t, b_im_t):
    g = a_re.shape[0]
    full2 = lambda s: pl.BlockSpec(s, lambda: (0,) * len(s))
    return pl.pallas_call(
        _ssm_prep_kernel,
        in_specs=[full2((g, SSM_STATE)), full2((g, SSM_STATE)), full2((g, 1)),
                  full2((g, SSM_CG, SSM_STATE)), full2((g, SSM_CG, SSM_STATE))],
        out_specs=[full2((g, SSM_STATE)), full2((g, SSM_STATE)),
                   full2((g, SSM_CG, SSM_STATE)), full2((g, SSM_CG, SSM_STATE))],
        out_shape=[jax.ShapeDtypeStruct((g, SSM_STATE), F32)] * 2
        + [jax.ShapeDtypeStruct((g, SSM_CG, SSM_STATE), F32)] * 2,
        name="ssm_prep",
    )(a_re, a_im, log_dt, b_re_t, b_im_t)


def _ssm_tail(s_re, s_im, u, wcr, wci, d, wglu):
    y = (jnp.dot(s_re.astype(BF16), wcr, preferred_element_type=F32)
         - jnp.dot(s_im.astype(BF16), wci, preferred_element_type=F32)
         + d * u)
    z = jnp.dot(_gelu_tanh(y).astype(BF16), wglu, preferred_element_type=F32)
    return z[:, :SLAB_CH] * jax.nn.sigmoid(z[:, SLAB_CH:])


def _ssm_prompt_kernel(u_ref, wb_ref, a_ref, wcr_ref, wci_ref, d_ref, wglu_ref,
                       o_ref, sre_ref, sim_ref, st_ref, carry_ref):
    nb = u_ref.shape[0]
    c = pl.program_id(1)

    @pl.when(c == 0)
    def _():
        carry_ref[...] = jnp.zeros_like(carry_ref)

    for b in range(nb):
        bu = jnp.dot(u_ref[b].astype(BF16), wb_ref[0], preferred_element_type=F32)
        for k in range(2 * ST_TILES):
            st_ref[b, pl.ds(k * SSM_PITCH, SSM_CHUNK), :] = bu[:, LANES * k:LANES * (k + 1)]

    a_re = a_ref[0, 0]
    a_im = a_ref[0, 1]
    im0 = ST_TILES * SSM_PITCH

    def step(t, carry):
        new = []
        for b in range(nb):
            sr, si = carry[2 * b], carry[2 * b + 1]
            dr = st_ref[b, pl.ds(t, SUBLANES, stride=SSM_PITCH), :]
            di = st_ref[b, pl.ds(im0 + t, SUBLANES, stride=SSM_PITCH), :]
            nr = a_re * sr - a_im * si + dr
            ni = a_re * si + a_im * sr + di
            st_ref[b, pl.ds(t, SUBLANES, stride=SSM_PITCH), :] = nr
            st_ref[b, pl.ds(im0 + t, SUBLANES, stride=SSM_PITCH), :] = ni
            new += [nr, ni]
        return tuple(new)

    init = tuple(carry_ref[b, j] for b in range(nb) for j in range(2))
    fin = lax.fori_loop(0, SSM_CHUNK, step, init)
    for b in range(nb):
        carry_ref[b, 0] = fin[2 * b]
        carry_ref[b, 1] = fin[2 * b + 1]
        sre_ref[0, b] = fin[2 * b]
        sim_ref[0, b] = fin[2 * b + 1]

    for b in range(nb):
        s_re = jnp.concatenate(
            [st_ref[b, pl.ds(k * SSM_PITCH, SSM_CHUNK), :] for k in range(ST_TILES)], axis=1)
        s_im = jnp.concatenate(
            [st_ref[b, pl.ds(im0 + k * SSM_PITCH, SSM_CHUNK), :] for k in range(ST_TILES)], axis=1)
        o_ref[b] = _ssm_tail(s_re, s_im, u_ref[b], wcr_ref[0], wci_ref[0], d_ref[0], wglu_ref[0])


def _ssm_prompt(z3, wb, a_dense, wcr, wci, d_row, wglu):
    nb, t, _ = z3.shape
    ublk = (Q_WIDTH + 2 * KV_WIDTH) // SLAB_CH
    return pl.pallas_call(
        _ssm_prompt_kernel,
        grid=(N_SLABS, t // SSM_CHUNK),
        in_specs=[
            pl.BlockSpec((nb, SSM_CHUNK, SLAB_CH), lambda s, c: (0, c, ublk + s)),
            pl.BlockSpec((1, SLAB_CH, 2 * SLAB_ST), lambda s, c: (s, 0, 0)),
            pl.BlockSpec((1, 2, SUBLANES, LANES), lambda s, c: (s, 0, 0, 0)),
            pl.BlockSpec((1, SLAB_ST, SLAB_CH), lambda s, c: (s, 0, 0)),
            pl.BlockSpec((1, SLAB_ST, SLAB_CH), lambda s, c: (s, 0, 0)),
            pl.BlockSpec((1, 1, SLAB_CH), lambda s, c: (s, 0, 0)),
            pl.BlockSpec((1, SLAB_CH, 2 * SLAB_CH), lambda s, c: (s, 0, 0)),
        ],
        out_specs=[
            pl.BlockSpec((nb, SSM_CHUNK, SLAB_CH), lambda s, c: (0, c, s)),
            pl.BlockSpec((1, nb, SUBLANES, LANES), lambda s, c: (s, 0, 0, 0)),
            pl.BlockSpec((1, nb, SUBLANES, LANES), lambda s, c: (s, 0, 0, 0)),
        ],
        out_shape=[
            jax.ShapeDtypeStruct((nb, t, SSM_WIDTH), F32),
            jax.ShapeDtypeStruct((N_SLABS, nb, SUBLANES, LANES), F32),
            jax.ShapeDtypeStruct((N_SLABS, nb, SUBLANES, LANES), F32),
        ],
        scratch_shapes=[
            pltpu.VMEM((nb, 2 * ST_TILES * SSM_PITCH, LANES), F32),
            pltpu.VMEM((nb, 2, SUBLANES, LANES), F32),
        ],
        compiler_params=_params(("parallel", "arbitrary")),
        name="ssm_prompt",
    )(z3, wb, a_dense, wcr, wci, d_row, wglu)


def _ssm_sample_kernel(u_ref, wb_ref, a_ref, s0r_ref, s0i_ref, wcr_ref, wci_ref, d_ref, wglu_ref,
                       o_ref, sre_ref, sim_ref):
    u = u_ref[...]
    bu = jnp.dot(u.astype(BF16), wb_ref[0], preferred_element_type=F32)
    a_re = a_ref[0, 0]
    a_im = a_ref[0, 1]
    s0r = s0r_ref[...]
    s0i = s0i_ref[...]
    nr = bu[:, :SLAB_ST] + a_re * s0r - a_im * s0i
    ni = bu[:, SLAB_ST:] + a_re * s0i + a_im * s0r
    sre_ref[...] = nr
    sim_ref[...] = ni
    o_ref[...] = _ssm_tail(nr, ni, u, wcr_ref[0], wci_ref[0], d_ref[0], wglu_ref[0])


def _ssm_sample(z, wb, a_row, s0r, s0i, wcr, wci, d_row, wglu):
    n = z.shape[0]
    ublk = (Q_WIDTH + 2 * KV_WIDTH) // SLAB_CH
    return pl.pallas_call(
        _ssm_sample_kernel,
        grid=(N_SLABS,),
        in_specs=[
            pl.BlockSpec((n, SLAB_CH), lambda s: (0, ublk + s)),
            pl.BlockSpec((1, SLAB_CH, 2 * SLAB_ST), lambda s: (s, 0, 0)),
            pl.BlockSpec((1, 2, 1, SLAB_ST), lambda s: (s, 0, 0, 0)),
            pl.BlockSpec((n, SLAB_ST), lambda s: (0, s)),
            pl.BlockSpec((n, SLAB_ST), lambda s: (0, s)),
            pl.BlockSpec((1, SLAB_ST, SLAB_CH), lambda s: (s, 0, 0)),
            pl.BlockSpec((1, SLAB_ST, SLAB_CH), lambda s: (s, 0, 0)),
            pl.BlockSpec((1, 1, SLAB_CH), lambda s: (s, 0, 0)),
            pl.BlockSpec((1, SLAB_CH, 2 * SLAB_CH), lambda s: (s, 0, 0)),
        ],
        out_specs=[
            pl.BlockSpec((n, SLAB_CH), lambda s: (0, s)),
            pl.BlockSpec((n, SLAB_ST), lambda s: (0, s)),
            pl.BlockSpec((n, SLAB_ST), lambda s: (0, s)),
        ],
        out_shape=[
            jax.ShapeDtypeStruct((n, SSM_WIDTH), F32),
            jax.ShapeDtypeStruct((n, SSM_GROUPS * SSM_STATE), F32),
            jax.ShapeDtypeStruct((n, SSM_GROUPS * SSM_STATE), F32),
        ],
        compiler_params=_params(("parallel",)),
        name="ssm_sample",
    )(z, wb, a_row, s0r, s0i, wcr, wci, d_row, wglu)


def _outproj_kernel(a_ref, s_ref, ga_ref, gs_ref, w_ref, gp_ref, x_ref, o_ref, m_ref, acc_ref):
    j = pl.program_id(1)
    nj = pl.num_programs(1)
    tn = w_ref.shape[1]

    @pl.when(j == 0)
    def _():
        m_ref[:, :Q_WIDTH] = _rms(a_ref[...], ga_ref[...]).astype(BF16)
        m_ref[:, Q_WIDTH:] = _rms(s_ref[...], gs_ref[...]).astype(BF16)

    acc_ref[j] = jnp.dot(m_ref[...], w_ref[...].astype(BF16), preferred_element_type=F32)

    @pl.when(j == nj - 1)
    def _():
        n_chunks = acc_ref.shape[0]
        ss = jnp.zeros((acc_ref.shape[1], 1), F32)
        for c in range(n_chunks):
            y = acc_ref[c]
            ss = ss + jnp.sum(y * y, axis=-1, keepdims=True)
        r = lax.rsqrt(ss / D_MODEL + EPS)
        for c in range(n_chunks):
            sl = slice(c * tn, (c + 1) * tn)
            o_ref[:, sl] = x_ref[:, sl] + acc_ref[c] * r * gp_ref[:, sl]


def _outproj(attn, ssm, ga, gs, w, gp, x, tm, tn=256):
    m = x.shape[0]
    nj = D_MODEL // tn
    return pl.pallas_call(
        _outproj_kernel,
        grid=(m // tm, nj),
        in_specs=[
            pl.BlockSpec((tm, Q_WIDTH), lambda i, j: (i, 0)),
            pl.BlockSpec((tm, SSM_WIDTH), lambda i, j: (i, 0)),
            pl.BlockSpec((1, Q_WIDTH), lambda i, j: (0, 0)),
            pl.BlockSpec((1, SSM_WIDTH), lambda i, j: (0, 0)),
            pl.BlockSpec((D_MODEL, tn), lambda i, j: (0, j)),
            pl.BlockSpec((1, D_MODEL), lambda i, j: (0, 0)),
            pl.BlockSpec((tm, D_MODEL), lambda i, j: (i, 0)),
        ],
        out_specs=pl.BlockSpec((tm, D_MODEL), lambda i, j: (i, 0)),
        out_shape=jax.ShapeDtypeStruct((m, D_MODEL), F32),
        scratch_shapes=[pltpu.VMEM((tm, D_MODEL), BF16), pltpu.VMEM((nj, tm, tn), F32)],
        compiler_params=_params(("parallel", "arbitrary")),
        name="outproj",
    )(attn, ssm, ga, gs, w, gp, x)


def _ffn_kernel(x_ref, g1_ref, wg_ref, wu_ref, wd_ref, g2_ref, o_ref, h_ref):
    j = pl.program_id(1)
    nj = pl.num_programs(1)

    @pl.when(j == 0)
    def _():
        h_ref[...] = _rms(x_ref[...], g1_ref[...]).astype(BF16)
        o_ref[...] = jnp.zeros_like(o_ref)

    h = h_ref[...]
    gate = jnp.dot(h, wg_ref[...].astype(BF16), preferred_element_type=F32)
    up = jnp.dot(h, wu_ref[...].astype(BF16), preferred_element_type=F32)
    act = (gate * jax.nn.sigmoid(gate) * up).astype(BF16)
    o_ref[...] += jnp.dot(act, wd_ref[...].astype(BF16), preferred_element_type=F32)

    @pl.when(j == nj - 1)
    def _():
        o_ref[...] = x_ref[...] + _rms(o_ref[...], g2_ref[...])


def _ffn(x, g1, wgu, wd, g2, tm, tf=256):
    m = x.shape[0]
    nf = D_FF // tf
    return pl.pallas_call(
        _ffn_kernel,
        grid=(m // tm, nf),
        in_specs=[
            pl.BlockSpec((tm, D_MODEL), lambda i, j: (i, 0), pipeline_mode=pl.Buffered(1)),
            pl.BlockSpec((1, D_MODEL), lambda i, j: (0, 0)),
            pl.BlockSpec((D_MODEL, tf), lambda i, j: (0, j)),
            pl.BlockSpec((D_MODEL, tf), lambda i, j: (0, nf + j)),
            pl.BlockSpec((tf, D_MODEL), lambda i, j: (j, 0)),
            pl.BlockSpec((1, D_MODEL), lambda i, j: (0, 0)),
        ],
        out_specs=pl.BlockSpec((tm, D_MODEL), lambda i, j: (i, 0)),
        out_shape=jax.ShapeDtypeStruct((m, D_MODEL), F32),
        scratch_shapes=[pltpu.VMEM((tm, D_MODEL), BF16)],
        compiler_params=_params(("parallel", "arbitrary")),
        name="ffn",
    )(x, g1, wgu, wgu, wd, g2)


def _ple_kernel(x_ref, xc_ref, pe_ref, wg_ref, wp_ref, o_ref, h_ref, p_ref):
    @pl.when(pl.program_id(1) == 0)
    def _():
        h_ref[...] = x_ref[...].astype(BF16)
        p_ref[...] = pe_ref[...].astype(BF16)

    gate = jnp.dot(h_ref[...], wg_ref[...].astype(BF16), preferred_element_type=F32)
    proj = jnp.dot(p_ref[...], wp_ref[...].astype(BF16), preferred_element_type=F32)
    o_ref[...] = xc_ref[...] + jax.nn.sigmoid(gate) * proj


def _ple(x, pe, wg, wp, tm, tn=256):
    m = x.shape[0]
    return pl.pallas_call(
        _ple_kernel,
        grid=(m // tm, D_MODEL // tn),
        in_specs=[
            pl.BlockSpec((tm, D_MODEL), lambda i, j: (i, 0)),
            pl.BlockSpec((tm, tn), lambda i, j: (i, j)),
            pl.BlockSpec((tm, PLE_DIM), lambda i, j: (i, 0)),
            pl.BlockSpec((D_MODEL, tn), lambda i, j: (0, j)),
            pl.BlockSpec((PLE_DIM, tn), lambda i, j: (0, j)),
        ],
        out_specs=pl.BlockSpec((tm, tn), lambda i, j: (i, j)),
        out_shape=jax.ShapeDtypeStruct((m, D_MODEL), F32),
        scratch_shapes=[pltpu.VMEM((tm, D_MODEL), BF16), pltpu.VMEM((tm, PLE_DIM), BF16)],
        compiler_params=_params(("parallel", "arbitrary")),
        name="ple",
    )(x, x, pe, wg, wp)


def _blockdiag(x):
    eye = jnp.eye(SLAB_GROUPS, dtype=bool)
    y = jnp.where(eye[:, None, :, None], x[..., :, :, None, :], jnp.zeros((), x.dtype))
    return y.reshape(x.shape[:-3] + (SLAB_GROUPS * x.shape[-2], SLAB_GROUPS * x.shape[-1]))


def _slabbed(x):
    return x.reshape((DEPTH, N_SLABS, SLAB_GROUPS) + x.shape[2:])


def kernel(x_prompt, x_sample, cache_k, cache_v, state_ssm_re, state_ssm_im, p_prompt, p_sample,
           g_pre_mix, w_in, attn_sinks, ssm_a_re, ssm_a_im, ssm_log_dt, ssm_b_re, ssm_b_im,
           ssm_c_re, ssm_c_im, ssm_d, ssm_w_glu, g_attn_out, g_ssm_out, w_out, g_post_mix,
           g_pre_ffn, w_gate_up, w_down, g_post_ffn, w_ple_gate, w_ple_proj):
    nb, t, _ = x_prompt.shape
    ns = x_sample.shape[0]
    tm_p = 1024
    tm_s = ns

    g_all = DEPTH * SSM_GROUPS
    ab_re, ab_im, bb_re, bb_im = _ssm_prep(
        ssm_a_re.reshape(g_all, SSM_STATE), ssm_a_im.reshape(g_all, SSM_STATE),
        ssm_log_dt.reshape(g_all, 1),
        jnp.swapaxes(ssm_b_re, -1, -2).reshape(g_all, SSM_CG, SSM_STATE),
        jnp.swapaxes(ssm_b_im, -1, -2).reshape(g_all, SSM_CG, SSM_STATE))
    ab = jnp.stack([ab_re.reshape(DEPTH, N_SLABS, SLAB_ST), ab_im.reshape(DEPTH, N_SLABS, SLAB_ST)], axis=2)
    a_dense = ab.reshape(DEPTH, N_SLABS, 2, SUBLANES, LANES)
    a_row = ab.reshape(DEPTH, N_SLABS, 2, 1, SLAB_ST)
    bb_re = _slabbed(bb_re.reshape(DEPTH, SSM_GROUPS, SSM_CG, SSM_STATE))
    bb_im = _slabbed(bb_im.reshape(DEPTH, SSM_GROUPS, SSM_CG, SSM_STATE))
    wb = jnp.concatenate([_blockdiag(bb_re), _blockdiag(bb_im)], axis=-1).astype(BF16)
    wcr = _blockdiag(_slabbed(jnp.swapaxes(ssm_c_re, -1, -2))).astype(BF16)
    wci = _blockdiag(_slabbed(jnp.swapaxes(ssm_c_im, -1, -2))).astype(BF16)
    d_row = ssm_d.reshape(DEPTH, N_SLABS, 1, SLAB_CH)
    wglu = jnp.concatenate([_blockdiag(_slabbed(ssm_w_glu[..., :SSM_CG])),
                            _blockdiag(_slabbed(ssm_w_glu[..., SSM_CG:]))], axis=-1).astype(BF16)

    slope_col = jnp.asarray(_SLOPES, F32).reshape(N_HEADS, 1)

    xp = x_prompt.reshape(nb * t, D_MODEL)
    xs = x_sample.reshape(ns, D_MODEL)
    pp = p_prompt.reshape(DEPTH, nb * t, PLE_DIM)
    ps = p_sample.reshape(DEPTH, ns, PLE_DIM)
    s0r = state_ssm_re.reshape(DEPTH, ns, SSM_GROUPS * SSM_STATE)
    s0i = state_ssm_im.reshape(DEPTH, ns, SSM_GROUPS * SSM_STATE)
    ck = cache_k.reshape(DEPTH, ns, WINDOW, KV_WIDTH)
    cv = cache_v.reshape(DEPTH, ns, WINDOW, KV_WIDTH)

    kp_l, vp_l, srp_l, sip_l = [], [], [], []
    ks_l, vs_l, srs_l, sis_l = [], [], [], []
    for l in range(DEPTH):
        row = lambda a: a[l].reshape(1, -1)
        z = _inproj(xp, row(g_pre_mix), w_in[l], tm_p)
        z3 = z.reshape(nb, t, IN_WIDTH)
        attn = _attn_prompt(z3, attn_sinks[l].reshape(1, N_HEADS))
        ssm, sre, sim = _ssm_prompt(z3, wb[l], a_dense[l], wcr[l], wci[l], d_row[l], wglu[l])
        x1 = _outproj(attn.reshape(nb * t, Q_WIDTH), ssm.reshape(nb * t, SSM_WIDTH),
                      row(g_attn_out), row(g_ssm_out), w_out[l], row(g_post_mix), xp, 512)
        x2 = _ffn(x1, row(g_pre_ffn), w_gate_up[l], w_down[l], row(g_post_ffn), tm_p)
        xp = _ple(x2, pp[l], w_ple_gate[l], w_ple_proj[l], tm_p)
        kp_l.append(z3[:, t - WINDOW:, Q_WIDTH:Q_WIDTH + KV_WIDTH].reshape(nb, WINDOW, N_KV, HEAD_DIM))
        vp_l.append(z3[:, t - WINDOW:, Q_WIDTH + KV_WIDTH:Q_WIDTH + 2 * KV_WIDTH].reshape(nb, WINDOW, N_KV, HEAD_DIM))
        unslab = lambda s: jnp.swapaxes(s, 0, 1).reshape(nb, SSM_GROUPS, SSM_STATE)
        srp_l.append(unslab(sre))
        sip_l.append(unslab(sim))

        zs = _inproj(xs, row(g_pre_mix), w_in[l], tm_s)
        kn = zs[:, Q_WIDTH:Q_WIDTH + KV_WIDTH]
        vn = zs[:, Q_WIDTH + KV_WIDTH:Q_WIDTH + 2 * KV_WIDTH]
        attn_s = _attn_sample(zs[:, :Q_WIDTH].reshape(ns, N_HEADS, HEAD_DIM),
                              kn.reshape(ns, 1, KV_WIDTH), vn.reshape(ns, 1, KV_WIDTH),
                              ck[l], cv[l], attn_sinks[l].reshape(N_HEADS, 1), slope_col)
        ssm_s, srs, sis = _ssm_sample(zs, wb[l], a_row[l], s0r[l], s0i[l], wcr[l], wci[l], d_row[l], wglu[l])
        x1s = _outproj(attn_s.reshape(ns, Q_WIDTH), ssm_s, row(g_attn_out), row(g_ssm_out),
                       w_out[l], row(g_post_mix), xs, tm_s)
        x2s = _ffn(x1s, row(g_pre_ffn), w_gate_up[l], w_down[l], row(g_post_ffn), tm_s)
        xs = _ple(x2s, ps[l], w_ple_gate[l], w_ple_proj[l], tm_s)
        ks_l.append(jnp.concatenate([ck[l][:, 1:], kn[:, None, :]], axis=1).reshape(ns, WINDOW, N_KV, HEAD_DIM))
        vs_l.append(jnp.concatenate([cv[l][:, 1:], vn[:, None, :]], axis=1).reshape(ns, WINDOW, N_KV, HEAD_DIM))
        srs_l.append(srs.reshape(ns, SSM_GROUPS, SSM_STATE))
        sis_l.append(sis.reshape(ns, SSM_GROUPS, SSM_STATE))

    return (xp.reshape(nb, t, D_MODEL), xs.reshape(ns, 1, D_MODEL),
            jnp.stack(kp_l), jnp.stack(vp_l), jnp.stack(srp_l), jnp.stack(sip_l),
            jnp.stack(ks_l), jnp.stack(vs_l), jnp.stack(srs_l), jnp.stack(sis_l))
```

```python
import functools
import math

import numpy as np
import jax
import jax.numpy as jnp
from jax import lax
from jax.experimental import pallas as pl
from jax.experimental.pallas import tpu as pltpu

F32 = jnp.float32
BF16 = jnp.bfloat16

D_MODEL = 2048
DEPTH = 4
HEAD_DIM = 64
N_HEADS = 16
N_KV = 2
GRP = N_HEADS // N_KV
WINDOW = 128
Q_WIDTH = N_HEADS * HEAD_DIM
KV_WIDTH = N_KV * HEAD_DIM
SSM_WIDTH = 1024
SSM_CG = 16
SSM_GROUPS = 64
SSM_STATE = 64
IN_WIDTH = Q_WIDTH + 2 * KV_WIDTH + SSM_WIDTH
D_FF = 5632
PLE_DIM = 256
EPS = 1e-6
NEG_INF = -1e30

LANES = 128
SUBLANES = 8

SLAB_GROUPS = 16
N_SLABS = SSM_GROUPS // SLAB_GROUPS
SLAB_CH = SLAB_GROUPS * SSM_CG
SLAB_ST = SLAB_GROUPS * SSM_STATE
ST_TILES = SLAB_ST // LANES
SSM_CHUNK = 256
SSM_PITCH = SSM_CHUNK + SUBLANES

TM_RES = 512
TN_W = 256
TM_FFN = 1024
TF_FFN = 256
SAMPLE_PER_STEP = 8

VMEM_LIMIT = 56 * 1024 * 1024

_SLOPES = [float(v) for v in
           (2.0 ** (-8.0 * np.arange(1, N_HEADS + 1, dtype=np.float32) / N_HEADS)).astype(np.float32)]


def _rms(x, g):
    var = jnp.mean(x * x, axis=-1, keepdims=True)
    return x * lax.rsqrt(var + EPS) * g


def _gelu_tanh(x):
    c = math.sqrt(2.0 / math.pi)
    return 0.5 * x * (1.0 + jnp.tanh(c * (x + 0.044715 * (x * x * x))))


def _params(sem):
    return pltpu.CompilerParams(dimension_semantics=sem, vmem_limit_bytes=VMEM_LIMIT)


def _tile_index(s, nw, nt):
    return jnp.minimum(jnp.maximum(s - nw, 0), nt - 1)


def _load_weight_chunks(s, w_ref, wres_ref, nw):
    tn = w_ref.shape[1]
    for c in range(nw):
        @pl.when(s == c)
        def _(c=c):
            wres_ref[:, c * tn:(c + 1) * tn] = w_ref[...].astype(BF16)


def _inproj_kernel(x_ref, xs_ref, g_ref, w_ref, o_ref, os_ref, wres_ref, *, nw, nt):
    s = pl.program_id(0)
    _load_weight_chunks(s, w_ref, wres_ref, nw)

    @pl.when((s >= nw) & (s < nw + nt))
    def _():
        h = _rms(x_ref[...], g_ref[...]).astype(BF16)
        o_ref[...] = jnp.dot(h, wres_ref[...], preferred_element_type=F32)

    @pl.when(s == nw + nt)
    def _():
        h = _rms(xs_ref[...], g_ref[...]).astype(BF16)
        os_ref[...] = jnp.dot(h, wres_ref[...], preferred_element_type=F32)


def _inproj(l, x, xs, g, w):
    m, ns = x.shape[0], xs.shape[0]
    n = w.shape[2]
    nw, nt = n // TN_W, m // TM_RES
    tile = lambda s: (_tile_index(s, nw, nt), 0)
    const = lambda s: (0, 0)
    return pl.pallas_call(
        functools.partial(_inproj_kernel, nw=nw, nt=nt),
        grid=(nw + nt + 1,),
        in_specs=[
            pl.BlockSpec((TM_RES, D_MODEL), tile),
            pl.BlockSpec((ns, D_MODEL), const),
            pl.BlockSpec((None, 1, D_MODEL), lambda s: (l, 0, 0)),
            pl.BlockSpec((None, D_MODEL, TN_W), lambda s: (l, 0, jnp.minimum(s, nw - 1))),
        ],
        out_specs=[pl.BlockSpec((TM_RES, n), tile), pl.BlockSpec((ns, n), const)],
        out_shape=[jax.ShapeDtypeStruct((m, n), F32), jax.ShapeDtypeStruct((ns, n), F32)],
        scratch_shapes=[pltpu.VMEM((D_MODEL, n), BF16)],
        compiler_params=_params(("arbitrary",)),
        name="inproj",
    )(x, xs, g, w)


def _attn_prompt_kernel(sink_ref, q_ref, kp_ref, kc_ref, vp_ref, vc_ref, o_ref, *, l):
    blk = pl.program_id(1)
    kcat = jnp.concatenate([kp_ref[...], kc_ref[...]], axis=0).astype(BF16)
    vcat = jnp.concatenate([vp_ref[...], vc_ref[...]], axis=0).astype(BF16)
    qi = lax.broadcasted_iota(jnp.int32, (WINDOW, 2 * WINDOW), 0)
    kj = lax.broadcasted_iota(jnp.int32, (WINDOW, 2 * WINDOW), 1)
    dist_i = qi - kj + WINDOW
    valid = (dist_i >= 0) & (dist_i <= WINDOW) & ((blk > 0) | (kj >= WINDOW))
    dist = dist_i.astype(F32)
    for h in range(N_HEADS):
        kv = h // GRP
        qh = q_ref[:, HEAD_DIM * h:HEAD_DIM * (h + 1)].astype(BF16)
        kh = kcat[:, HEAD_DIM * kv:HEAD_DIM * (kv + 1)]
        vh = vcat[:, HEAD_DIM * kv:HEAD_DIM * (kv + 1)]
        s = lax.dot_general(qh, kh, (((1,), (1,)), ((), ())), preferred_element_type=F32)
        s = s * (HEAD_DIM ** -0.5)
        s = jnp.where(valid, s - _SLOPES[h] * dist, NEG_INF)
        sink = sink_ref[l, h]
        m = jnp.maximum(jnp.max(s, axis=-1, keepdims=True), sink)
        e = jnp.exp(s - m)
        denom = jnp.sum(e, axis=-1, keepdims=True) + jnp.exp(sink - m)
        oh = jnp.dot(e.astype(BF16), vh, preferred_element_type=F32) / denom
        o_ref[:, HEAD_DIM * h:HEAD_DIM * (h + 1)] = oh


def _attn_prompt(l, z, sinks, nb, t):
    nblk = t // WINDOW
    kcol = Q_WIDTH // KV_WIDTH
    cur = lambda c: (lambda b, i: (b * nblk + i, c))
    prev = lambda c: (lambda b, i: (b * nblk + jnp.maximum(i - 1, 0), c))
    return pl.pallas_call(
        functools.partial(_attn_prompt_kernel, l=l),
        grid=(nb, nblk),
        in_specs=[
            pl.BlockSpec(memory_space=pltpu.SMEM),
            pl.BlockSpec((WINDOW, Q_WIDTH), cur(0)),
            pl.BlockSpec((WINDOW, KV_WIDTH), prev(kcol)),
            pl.BlockSpec((WINDOW, KV_WIDTH), cur(kcol)),
            pl.BlockSpec((WINDOW, KV_WIDTH), prev(kcol + 1)),
            pl.BlockSpec((WINDOW, KV_WIDTH), cur(kcol + 1)),
        ],
        out_specs=pl.BlockSpec((WINDOW, Q_WIDTH), cur(0)),
        out_shape=jax.ShapeDtypeStruct((nb * t, Q_WIDTH), F32),
        compiler_params=_params(("arbitrary", "arbitrary")),
        name="attn_prompt",
    )(sinks, z, z, z, z, z)


def _attn_sample_kernel(q_ref, kn_ref, vn_ref, ck_ref, cv_ref, sink_ref, slope_ref, o_ref):
    dist = (WINDOW - lax.broadcasted_iota(jnp.int32, (GRP, WINDOW), 1)).astype(F32)
    for n in range(q_ref.shape[0]):
        ck = ck_ref[n].astype(BF16)
        cv = cv_ref[n].astype(BF16)
        kn = kn_ref[n]
        vn = vn_ref[n]
        for kv in range(N_KV):
            lo, hi = HEAD_DIM * kv, HEAD_DIM * (kv + 1)
            qk = q_ref[n, GRP * kv:GRP * (kv + 1), :]
            slope = slope_ref[GRP * kv:GRP * (kv + 1), :]
            sink = sink_ref[GRP * kv:GRP * (kv + 1), :]
            s_c = lax.dot_general(qk.astype(BF16), ck[:, lo:hi], (((1,), (1,)), ((), ())),
                                  preferred_element_type=F32) * (HEAD_DIM ** -0.5)
            s_c = s_c - slope * dist
            s_n = jnp.sum(qk * kn[:, lo:hi], axis=-1, keepdims=True) * (HEAD_DIM ** -0.5)
            m = jnp.maximum(jnp.maximum(jnp.max(s_c, axis=-1, keepdims=True), s_n), sink)
            e_c = jnp.exp(s_c - m)
            e_n = jnp.exp(s_n - m)
            denom = jnp.sum(e_c, axis=-1, keepdims=True) + e_n + jnp.exp(sink - m)
            o = jnp.dot(e_c.astype(BF16), cv[:, lo:hi], preferred_element_type=F32) + e_n * vn[:, lo:hi]
            o_ref[n, GRP * kv:GRP * (kv + 1), :] = o / denom


def _attn_sample(l, q3, kn, vn, ck, cv, sink_col, slope_col):
    n = q3.shape[0]
    sp = SAMPLE_PER_STEP
    return pl.pallas_call(
        _attn_sample_kernel,
        grid=(n // sp,),
        in_specs=[
            pl.BlockSpec((sp, N_HEADS, HEAD_DIM), lambda i: (i, 0, 0)),
            pl.BlockSpec((sp, 1, KV_WIDTH), lambda i: (i, 0, 0)),
            pl.BlockSpec((sp, 1, KV_WIDTH), lambda i: (i, 0, 0)),
            pl.BlockSpec((None, sp, WINDOW, KV_WIDTH), lambda i: (l, i, 0, 0)),
            pl.BlockSpec((None, sp, WINDOW, KV_WIDTH), lambda i: (l, i, 0, 0)),
            pl.BlockSpec((None, N_HEADS, 1), lambda i: (l, 0, 0)),
            pl.BlockSpec((N_HEADS, 1), lambda i: (0, 0)),
        ],
        out_specs=pl.BlockSpec((sp, N_HEADS, HEAD_DIM), lambda i: (i, 0, 0)),
        out_shape=jax.ShapeDtypeStruct((n, N_HEADS, HEAD_DIM), F32),
        compiler_params=_params(("arbitrary",)),
        name="attn_sample",
    )(q3, kn, vn, ck, cv, sink_col, slope_col)


def _discretise(a_re, a_im, log_dt):
    dt = jnp.exp(log_dt)
    dta_re = dt * a_re
    dta_im = dt * a_im
    mag = jnp.exp(dta_re)
    return mag * jnp.cos(dta_im), mag * jnp.sin(dta_im)


def _iota2(shape):
    return (lax.broadcasted_iota(jnp.int32, shape, 0), lax.broadcasted_iota(jnp.int32, shape, 1))


def _ssm_prep_kernel(are_ref, aim_ref, ldt_ref, bre_ref, bim_ref, cre_ref, cim_ref, wg_ref,
                     ared_ref, aimd_ref, ldtd_ref,
                     wb_ref, wcr_ref, wci_ref, wglu_ref, ad_ref, ar_ref):
    copy_dot = lambda a, b: jnp.dot(a.astype(BF16), b, preferred_element_type=F32)

    a_re = are_ref[...]
    a_im = aim_ref[...]
    ab_re, ab_im = _discretise(a_re, a_im, ldt_ref[...])
    ar_ref[0] = ab_re
    ar_ref[1] = ab_im
    den = a_re * a_re + a_im * a_im
    f_re = ((ab_re - 1.0) * a_re + ab_im * a_im) / den
    f_im = (ab_im * a_re - (ab_re - 1.0) * a_im) / den
    r, _ = _iota2((LANES, SLAB_ST))
    f_t = jnp.where(r == 0, f_re, jnp.where(r == 1, f_im, 0.0)).T
    f_re = f_t[:, 0:1]
    f_im = f_t[:, 1:2]

    b_re = bre_ref[...]
    b_im = bim_ref[...]
    r, c = _iota2((SSM_CG, SLAB_CH))
    tile_c = (r == (c & (SSM_CG - 1))).astype(BF16)
    r, c = _iota2((SLAB_ST, SLAB_CH))
    diag = (r >> 6) == (c >> 4)
    bbt_re = jnp.where(diag, copy_dot(f_re * b_re - f_im * b_im, tile_c), 0.0)
    bbt_im = jnp.where(diag, copy_dot(f_re * b_im + f_im * b_re, tile_c), 0.0)
    wb_ref[:, :SLAB_ST] = bbt_re.T.astype(BF16)
    wb_ref[:, SLAB_ST:] = bbt_im.T.astype(BF16)

    r, c = _iota2((SSM_STATE, SLAB_ST))
    tile_p = (r == (c & (SSM_STATE - 1))).astype(BF16)
    r, c = _iota2((SLAB_CH, SLAB_ST))
    diag = (r >> 4) == (c >> 6)
    wcr_ref[...] = jnp.where(diag, copy_dot(cre_ref[...], tile_p), 0.0).T.astype(BF16)
    wci_ref[...] = jnp.where(diag, copy_dot(cim_ref[...], tile_p), 0.0).T.astype(BF16)

    r, c = _iota2((2 * SSM_CG, 2 * SLAB_CH))
    tile_e = (r == (c & (SSM_CG - 1)) + SSM_CG * (c >> 8)).astype(BF16)
    r, c = _iota2((SLAB_CH, 2 * SLAB_CH))
    diag = (r >> 4) == ((c & (SLAB_CH - 1)) >> 4)
    wglu_ref[...] = jnp.where(diag, copy_dot(wg_ref[...], tile_e), 0.0).astype(BF16)

    abd_re, abd_im = _discretise(ared_ref[...], aimd_ref[...], ldtd_ref[...])
    ad_ref[0] = abd_re
    ad_ref[1] = abd_im


def _ssm_prep(a_re, a_im, log_dt, b_re, b_im, c_re, c_im, w_glu):
    gp = SSM_GROUPS * SSM_STATE
    gc = SSM_GROUPS * SSM_CG
    ldt_gp = jnp.repeat(log_dt, SSM_STATE, axis=1)
    row = lambda a: a.reshape(DEPTH, N_SLABS, 1, SLAB_ST)
    dense = lambda a: a.reshape(DEPTH, N_SLABS, SUBLANES, LANES)
    blk = lambda rows, cols: pl.BlockSpec((None, rows, cols), lambda l, s: (l, s, 0))
    rblk = pl.BlockSpec((None, None, 1, SLAB_ST), lambda l, s: (l, s, 0, 0))
    dblk = pl.BlockSpec((None, None, SUBLANES, LANES), lambda l, s: (l, s, 0, 0))
    oblk = lambda rows, cols: pl.BlockSpec((None, None, rows, cols), lambda l, s: (l, s, 0, 0))
    return pl.pallas_call(
        _ssm_prep_kernel,
        grid=(DEPTH, N_SLABS),
        in_specs=[rblk, rblk, rblk,
                  blk(SLAB_ST, SSM_CG), blk(SLAB_ST, SSM_CG),
                  blk(SLAB_CH, SSM_STATE), blk(SLAB_CH, SSM_STATE), blk(SLAB_CH, 2 * SSM_CG),
                  dblk, dblk, dblk],
        out_specs=[oblk(SLAB_CH, 2 * SLAB_ST), oblk(SLAB_ST, SLAB_CH), oblk(SLAB_ST, SLAB_CH),
                   oblk(SLAB_CH, 2 * SLAB_CH),
                   pl.BlockSpec((None, None, 2, SUBLANES, LANES), lambda l, s: (l, s, 0, 0, 0)),
                   pl.BlockSpec((None, None, 2, 1, SLAB_ST), lambda l, s: (l, s, 0, 0, 0))],
        out_shape=[jax.ShapeDtypeStruct((DEPTH, N_SLABS, SLAB_CH, 2 * SLAB_ST), BF16),
                   jax.ShapeDtypeStruct((DEPTH, N_SLABS, SLAB_ST, SLAB_CH), BF16),
                   jax.ShapeDtypeStruct((DEPTH, N_SLABS, SLAB_ST, SLAB_CH), BF16),
                   jax.ShapeDtypeStruct((DEPTH, N_SLABS, SLAB_CH, 2 * SLAB_CH), BF16),
                   jax.ShapeDtypeStruct((DEPTH, N_SLABS, 2, SUBLANES, LANES), F32),
                   jax.ShapeDtypeStruct((DEPTH, N_SLABS, 2, 1, SLAB_ST), F32)],
        compiler_params=_params(("arbitrary", "arbitrary")),
        name="ssm_prep",
    )(row(a_re), row(a_im), row(ldt_gp),
      b_re.reshape(DEPTH, gp, SSM_CG), b_im.reshape(DEPTH, gp, SSM_CG),
      c_re.reshape(DEPTH, gc, SSM_STATE), c_im.reshape(DEPTH, gc, SSM_STATE),
      w_glu.reshape(DEPTH, gc, 2 * SSM_CG),
      dense(a_re), dense(a_im), dense(ldt_gp))


def _ssm_tail(s_re, s_im, u, wcr, wci, d, wglu):
    y = (jnp.dot(s_re.astype(BF16), wcr, preferred_element_type=F32)
         - jnp.dot(s_im.astype(BF16), wci, preferred_element_type=F32)
         + d * u)
    z = jnp.dot(_gelu_tanh(y).astype(BF16), wglu, preferred_element_type=F32)
    return z[:, :SLAB_CH] * jax.nn.sigmoid(z[:, SLAB_CH:])


def _ssm_prompt_kernel(u_ref, wb_ref, a_ref, wcr_ref, wci_ref, d_ref, wglu_ref,
                       o_ref, sre_ref, sim_ref, st_ref, carry_ref):
    nb = u_ref.shape[0]
    c = pl.program_id(1)

    @pl.when(c == 0)
    def _():
        carry_ref[...] = jnp.zeros_like(carry_ref)

    for b in range(nb):
        bu = jnp.dot(u_ref[b].astype(BF16), wb_ref[...], preferred_element_type=F32)
        for k in range(2 * ST_TILES):
            st_ref[b, pl.ds(k * SSM_PITCH, SSM_CHUNK), :] = bu[:, LANES * k:LANES * (k + 1)]

    a_re = a_ref[0]
    a_im = a_ref[1]
    im0 = ST_TILES * SSM_PITCH

    def step(t, carry):
        new = []
        for b in range(nb):
            sr, si = carry[2 * b], carry[2 * b + 1]
            dr = st_ref[b, pl.ds(t, SUBLANES, stride=SSM_PITCH), :]
            di = st_ref[b, pl.ds(im0 + t, SUBLANES, stride=SSM_PITCH), :]
            nr = a_re * sr - a_im * si + dr
            ni = a_re * si + a_im * sr + di
            st_ref[b, pl.ds(t, SUBLANES, stride=SSM_PITCH), :] = nr
            st_ref[b, pl.ds(im0 + t, SUBLANES, stride=SSM_PITCH), :] = ni
            new += [nr, ni]
        return tuple(new)

    init = tuple(carry_ref[b, j] for b in range(nb) for j in range(2))
    fin = lax.fori_loop(0, SSM_CHUNK, step, init)
    for b in range(nb):
        carry_ref[b, 0] = fin[2 * b]
        carry_ref[b, 1] = fin[2 * b + 1]
        sre_ref[b] = fin[2 * b]
        sim_ref[b] = fin[2 * b + 1]

    for b in range(nb):
        s_re = jnp.concatenate(
            [st_ref[b, pl.ds(k * SSM_PITCH, SSM_CHUNK), :] for k in range(ST_TILES)], axis=1)
        s_im = jnp.concatenate(
            [st_ref[b, pl.ds(im0 + k * SSM_PITCH, SSM_CHUNK), :] for k in range(ST_TILES)], axis=1)
        o_ref[b] = _ssm_tail(s_re, s_im, u_ref[b], wcr_ref[...], wci_ref[...], d_ref[...], wglu_ref[...])


def _slab_specs(l):
    w4 = lambda rows, cols: pl.BlockSpec((None, None, rows, cols), lambda s, *_: (l, s, 0, 0))
    return dict(wb=w4(SLAB_CH, 2 * SLAB_ST), wc=w4(SLAB_ST, SLAB_CH),
                d=w4(1, SLAB_CH), wglu=w4(SLAB_CH, 2 * SLAB_CH))


def _ssm_prompt(l, z3, wb, a_dense, wcr, wci, d_row, wglu):
    nb, t, _ = z3.shape
    ucol = (Q_WIDTH + 2 * KV_WIDTH) // SLAB_CH
    sp = _slab_specs(l)
    st_blk = pl.BlockSpec((None, nb, SUBLANES, LANES), lambda s, c: (s, 0, 0, 0))
    return pl.pallas_call(
        _ssm_prompt_kernel,
        grid=(N_SLABS, t // SSM_CHUNK),
        in_specs=[
            pl.BlockSpec((nb, SSM_CHUNK, SLAB_CH), lambda s, c: (0, c, ucol + s)),
            sp["wb"],
            pl.BlockSpec((None, None, 2, SUBLANES, LANES), lambda s, c: (l, s, 0, 0, 0)),
            sp["wc"], sp["wc"], sp["d"], sp["wglu"],
        ],
        out_specs=[pl.BlockSpec((nb, SSM_CHUNK, SLAB_CH), lambda s, c: (0, c, s)), st_blk, st_blk],
        out_shape=[
            jax.ShapeDtypeStruct((nb, t, SSM_WIDTH), F32),
            jax.ShapeDtypeStruct((N_SLABS, nb, SUBLANES, LANES), F32),
            jax.ShapeDtypeStruct((N_SLABS, nb, SUBLANES, LANES), F32),
        ],
        scratch_shapes=[
            pltpu.VMEM((nb, 2 * ST_TILES * SSM_PITCH, LANES), F32),
            pltpu.VMEM((nb, 2, SUBLANES, LANES), F32),
        ],
        compiler_params=_params(("arbitrary", "arbitrary")),
        name="ssm_prompt",
    )(z3, wb, a_dense, wcr, wci, d_row, wglu)


def _ssm_sample_kernel(u_ref, wb_ref, a_ref, s0r_ref, s0i_ref, wcr_ref, wci_ref, d_ref, wglu_ref,
                       o_ref, sre_ref, sim_ref):
    u = u_ref[...]
    bu = jnp.dot(u.astype(BF16), wb_ref[...], preferred_element_type=F32)
    a_re = a_ref[0]
    a_im = a_ref[1]
    s0r = s0r_ref[...]
    s0i = s0i_ref[...]
    nr = bu[:, :SLAB_ST] + a_re * s0r - a_im * s0i
    ni = bu[:, SLAB_ST:] + a_re * s0i + a_im * s0r
    sre_ref[...] = nr
    sim_ref[...] = ni
    o_ref[...] = _ssm_tail(nr, ni, u, wcr_ref[...], wci_ref[...], d_ref[...], wglu_ref[...])


def _ssm_sample(l, z, wb, a_row, s0r, s0i, wcr, wci, d_row, wglu):
    n = z.shape[0]
    ucol = (Q_WIDTH + 2 * KV_WIDTH) // SLAB_CH
    sp = _slab_specs(l)
    st_in = pl.BlockSpec((None, n, SLAB_ST), lambda s: (l, 0, s))
    st_out = pl.BlockSpec((n, SLAB_ST), lambda s: (0, s))
    return pl.pallas_call(
        _ssm_sample_kernel,
        grid=(N_SLABS,),
        in_specs=[
            pl.BlockSpec((n, SLAB_CH), lambda s: (0, ucol + s)),
            sp["wb"],
            pl.BlockSpec((None, None, 2, 1, SLAB_ST), lambda s: (l, s, 0, 0, 0)),
            st_in, st_in, sp["wc"], sp["wc"], sp["d"], sp["wglu"],
        ],
        out_specs=[pl.BlockSpec((n, SLAB_CH), lambda s: (0, s)), st_out, st_out],
        out_shape=[
            jax.ShapeDtypeStruct((n, SSM_WIDTH), F32),
            jax.ShapeDtypeStruct((n, SSM_GROUPS * SSM_STATE), F32),
            jax.ShapeDtypeStruct((n, SSM_GROUPS * SSM_STATE), F32),
        ],
        compiler_params=_params(("arbitrary",)),
        name="ssm_sample",
    )(z, wb, a_row, s0r, s0i, wcr, wci, d_row, wglu)


def _outproj_kernel(a_ref, m_ref, x_ref, as_ref, ms_ref, xs_ref, ga_ref, gs_ref, gp_ref, w_ref,
                    o_ref, os_ref, wres_ref, *, nw, nt):
    s = pl.program_id(0)
    _load_weight_chunks(s, w_ref, wres_ref, nw)

    def tile(attn, ssm, x):
        merged = jnp.concatenate([_rms(attn, ga_ref[...]).astype(BF16),
                                  _rms(ssm, gs_ref[...]).astype(BF16)], axis=1)
        y = jnp.dot(merged, wres_ref[...], preferred_element_type=F32)
        return x + _rms(y, gp_ref[...])

    @pl.when((s >= nw) & (s < nw + nt))
    def _():
        o_ref[...] = tile(a_ref[...], m_ref[...], x_ref[...])

    @pl.when(s == nw + nt)
    def _():
        os_ref[...] = tile(as_ref[...], ms_ref[...], xs_ref[...])


def _outproj(l, attn, ssm, x, attn_s, ssm_s, xs, ga, gs, gp, w):
    m, ns = x.shape[0], xs.shape[0]
    nw, nt = D_MODEL // TN_W, m // TM_RES
    tile = lambda s: (_tile_index(s, nw, nt), 0)
    const = lambda s: (0, 0)
    gain = lambda n: pl.BlockSpec((None, 1, n), lambda s: (l, 0, 0))
    return pl.pallas_call(
        functools.partial(_outproj_kernel, nw=nw, nt=nt),
        grid=(nw + nt + 1,),
        in_specs=[
            pl.BlockSpec((TM_RES, Q_WIDTH), tile),
            pl.BlockSpec((TM_RES, SSM_WIDTH), tile),
            pl.BlockSpec((TM_RES, D_MODEL), tile),
            pl.BlockSpec((ns, Q_WIDTH), const),
            pl.BlockSpec((ns, SSM_WIDTH), const),
            pl.BlockSpec((ns, D_MODEL), const),
            gain(Q_WIDTH), gain(SSM_WIDTH), gain(D_MODEL),
            pl.BlockSpec((None, D_MODEL, TN_W), lambda s: (l, 0, jnp.minimum(s, nw - 1))),
        ],
        out_specs=[pl.BlockSpec((TM_RES, D_MODEL), tile), pl.BlockSpec((ns, D_MODEL), const)],
        out_shape=[jax.ShapeDtypeStruct((m, D_MODEL), F32), jax.ShapeDtypeStruct((ns, D_MODEL), F32)],
        scratch_shapes=[pltpu.VMEM((D_MODEL, D_MODEL), BF16)],
        compiler_params=_params(("arbitrary",)),
        name="outproj",
    )(attn, ssm, x, attn_s, ssm_s, xs, ga, gs, gp, w)


def _ffn_kernel(x_ref, xs_ref, g1_ref, wg_ref, wu_ref, wd_ref, g2_ref, o_ref, os_ref, h_ref):
    i = pl.program_id(0)
    j = pl.program_id(1)
    nj = pl.num_programs(1)
    tm = x_ref.shape[0]

    @pl.when(j == 0)
    def _():
        h_ref[:tm, :] = _rms(x_ref[...], g1_ref[...]).astype(BF16)
        h_ref[tm:, :] = _rms(xs_ref[...], g1_ref[...]).astype(BF16)
        o_ref[...] = jnp.zeros_like(o_ref)

    @pl.when((j == 0) & (i == 0))
    def _():
        os_ref[...] = jnp.zeros_like(os_ref)

    h = h_ref[...]
    gate = jnp.dot(h, wg_ref[...].astype(BF16), preferred_element_type=F32)
    up = jnp.dot(h, wu_ref[...].astype(BF16), preferred_element_type=F32)
    act = (gate * jax.nn.sigmoid(gate) * up).astype(BF16)
    down = jnp.dot(act, wd_ref[...].astype(BF16), preferred_element_type=F32)
    o_ref[...] += down[:tm]

    @pl.when(i == 0)
    def _():
        os_ref[...] += down[tm:]

    @pl.when(j == nj - 1)
    def _():
        o_ref[...] = x_ref[...] + _rms(o_ref[...], g2_ref[...])

    @pl.when((j == nj - 1) & (i == 0))
    def _():
        os_ref[...] = xs_ref[...] + _rms(os_ref[...], g2_ref[...])


def _ffn(l, x, xs, g1, wgu, wd, g2):
    m, ns = x.shape[0], xs.shape[0]
    nf = D_FF // TF_FFN
    gain = pl.BlockSpec((None, 1, D_MODEL), lambda i, j: (l, 0, 0))
    return pl.pallas_call(
        _ffn_kernel,
        grid=(m // TM_FFN, nf),
        in_specs=[
            pl.BlockSpec((TM_FFN, D_MODEL), lambda i, j: (i, 0), pipeline_mode=pl.Buffered(1)),
            pl.BlockSpec((ns, D_MODEL), lambda i, j: (0, 0)),
            gain,
            pl.BlockSpec((None, D_MODEL, TF_FFN), lambda i, j: (l, 0, j)),
            pl.BlockSpec((None, D_MODEL, TF_FFN), lambda i, j: (l, 0, nf + j)),
            pl.BlockSpec((None, TF_FFN, D_MODEL), lambda i, j: (l, j, 0)),
            gain,
        ],
        out_specs=[pl.BlockSpec((TM_FFN, D_MODEL), lambda i, j: (i, 0)),
                   pl.BlockSpec((ns, D_MODEL), lambda i, j: (0, 0))],
        out_shape=[jax.ShapeDtypeStruct((m, D_MODEL), F32), jax.ShapeDtypeStruct((ns, D_MODEL), F32)],
        scratch_shapes=[pltpu.VMEM((TM_FFN + ns, D_MODEL), BF16)],
        compiler_params=_params(("arbitrary", "arbitrary")),
        name="ffn",
    )(x, xs, g1, wgu, wgu, wd, g2)


def _ple_kernel(x_ref, pe_ref, xs_ref, pes_ref, wg_ref, wp_ref, o_ref, os_ref, wgres_ref, wpres_ref,
                *, nw, nt):
    s = pl.program_id(0)
    _load_weight_chunks(s, wg_ref, wgres_ref, nw)

    @pl.when(s == 0)
    def _():
        wpres_ref[...] = wp_ref[...].astype(BF16)

    def tile(x, pe):
        gate = jnp.dot(x.astype(BF16), wgres_ref[...], preferred_element_type=F32)
        proj = jnp.dot(pe.astype(BF16), wpres_ref[...], preferred_element_type=F32)
        return x + jax.nn.sigmoid(gate) * proj

    @pl.when((s >= nw) & (s < nw + nt))
    def _():
        o_ref[...] = tile(x_ref[...], pe_ref[...])

    @pl.when(s == nw + nt)
    def _():
        os_ref[...] = tile(xs_ref[...], pes_ref[...])


def _ple(l, x, pe, xs, pes, wg, wp):
    m, ns = x.shape[0], xs.shape[0]
    nw, nt = D_MODEL // TN_W, m // TM_RES
    tile = lambda s: (_tile_index(s, nw, nt), 0)
    const = lambda s: (0, 0)
    return pl.pallas_call(
        functools.partial(_ple_kernel, nw=nw, nt=nt),
        grid=(nw + nt + 1,),
        in_specs=[
            pl.BlockSpec((TM_RES, D_MODEL), tile),
            pl.BlockSpec((None, TM_RES, PLE_DIM), lambda s: (l, _tile_index(s, nw, nt), 0)),
            pl.BlockSpec((ns, D_MODEL), const),
            pl.BlockSpec((None, ns, PLE_DIM), lambda s: (l, 0, 0)),
            pl.BlockSpec((None, D_MODEL, TN_W), lambda s: (l, 0, jnp.minimum(s, nw - 1))),
            pl.BlockSpec((None, PLE_DIM, D_MODEL), lambda s: (l, 0, 0)),
        ],
        out_specs=[pl.BlockSpec((TM_RES, D_MODEL), tile), pl.BlockSpec((ns, D_MODEL), const)],
        out_shape=[jax.ShapeDtypeStruct((m, D_MODEL), F32), jax.ShapeDtypeStruct((ns, D_MODEL), F32)],
        scratch_shapes=[pltpu.VMEM((D_MODEL, D_MODEL), BF16), pltpu.VMEM((PLE_DIM, D_MODEL), BF16)],
        compiler_params=_params(("arbitrary",)),
        name="ple",
    )(x, pe, xs, pes, wg, wp)


def kernel(x_prompt, x_sample, cache_k, cache_v, state_ssm_re, state_ssm_im, p_prompt, p_sample,
           g_pre_mix, w_in, attn_sinks, ssm_a_re, ssm_a_im, ssm_log_dt, ssm_b_re, ssm_b_im,
           ssm_c_re, ssm_c_im, ssm_d, ssm_w_glu, g_attn_out, g_ssm_out, w_out, g_post_mix,
           g_pre_ffn, w_gate_up, w_down, g_post_ffn, w_ple_gate, w_ple_proj):
    nb, t, _ = x_prompt.shape
    ns = x_sample.shape[0]

    wb, wcr, wci, wglu, a_dense, a_row = _ssm_prep(ssm_a_re, ssm_a_im, ssm_log_dt, ssm_b_re, ssm_b_im,
                                                   ssm_c_re, ssm_c_im, ssm_w_glu)
    d_row = ssm_d.reshape(DEPTH, N_SLABS, 1, SLAB_CH)
    slope_col = jnp.asarray(_SLOPES, F32).reshape(N_HEADS, 1)
    sink_col = attn_sinks.reshape(DEPTH, N_HEADS, 1)
    gain = lambda g: g.reshape(DEPTH, 1, g.shape[-1])
    g_pre_mix, g_attn_out, g_ssm_out, g_post_mix, g_pre_ffn, g_post_ffn = map(
        gain, (g_pre_mix, g_attn_out, g_ssm_out, g_post_mix, g_pre_ffn, g_post_ffn))

    xp = x_prompt.reshape(nb * t, D_MODEL)
    xs = x_sample.reshape(ns, D_MODEL)
    pp = p_prompt.reshape(DEPTH, nb * t, PLE_DIM)
    ps = p_sample.reshape(DEPTH, ns, PLE_DIM)
    s0r = state_ssm_re.reshape(DEPTH, ns, SSM_GROUPS * SSM_STATE)
    s0i = state_ssm_im.reshape(DEPTH, ns, SSM_GROUPS * SSM_STATE)
    ck = cache_k.reshape(DEPTH, ns, WINDOW, KV_WIDTH)
    cv = cache_v.reshape(DEPTH, ns, WINDOW, KV_WIDTH)

    kp_l, vp_l, srp_l, sip_l = [], [], [], []
    ks_l, vs_l, srs_l, sis_l = [], [], [], []
    for l in range(DEPTH):
        z, zs = _inproj(l, xp, xs, g_pre_mix, w_in)
        z3 = z.reshape(nb, t, IN_WIDTH)

        attn = _attn_prompt(l, z, attn_sinks, nb, t)
        ssm, sre, sim = _ssm_prompt(l, z3, wb, a_dense, wcr, wci, d_row, wglu)

        kn = zs[:, Q_WIDTH:Q_WIDTH + KV_WIDTH]
        vn = zs[:, Q_WIDTH + KV_WIDTH:Q_WIDTH + 2 * KV_WIDTH]
        attn_s = _attn_sample(l, zs[:, :Q_WIDTH].reshape(ns, N_HEADS, HEAD_DIM),
                              kn.reshape(ns, 1, KV_WIDTH), vn.reshape(ns, 1, KV_WIDTH),
                              ck, cv, sink_col, slope_col)
        ssm_s, srs, sis = _ssm_sample(l, zs, wb, a_row, s0r, s0i, wcr, wci, d_row, wglu)

        x1, x1s = _outproj(l, attn, ssm.reshape(nb * t, SSM_WIDTH), xp,
                           attn_s.reshape(ns, Q_WIDTH), ssm_s, xs,
                           g_attn_out, g_ssm_out, g_post_mix, w_out)
        x2, x2s = _ffn(l, x1, x1s, g_pre_ffn, w_gate_up, w_down, g_post_ffn)
        xp, xs = _ple(l, x2, pp, x2s, ps, w_ple_gate, w_ple_proj)

        kp_l.append(z3[:, t - WINDOW:, Q_WIDTH:Q_WIDTH + KV_WIDTH].reshape(nb, WINDOW, N_KV, HEAD_DIM))
        vp_l.append(z3[:, t - WINDOW:, Q_WIDTH + KV_WIDTH:Q_WIDTH + 2 * KV_WIDTH].reshape(nb, WINDOW, N_KV, HEAD_DIM))
        unslab = lambda s: jnp.swapaxes(s, 0, 1).reshape(nb, SSM_GROUPS, SSM_STATE)
        srp_l.append(unslab(sre))
        sip_l.append(unslab(sim))
        ks_l.append(jnp.concatenate([ck[l][:, 1:], kn[:, None, :]], axis=1).reshape(ns, WINDOW, N_KV, HEAD_DIM))
        vs_l.append(jnp.concatenate([cv[l][:, 1:], vn[:, None, :]], axis=1).reshape(ns, WINDOW, N_KV, HEAD_DIM))
        srs_l.append(srs.reshape(ns, SSM_GROUPS, SSM_STATE))
        sis_l.append(sis.reshape(ns, SSM_GROUPS, SSM_STATE))

    return (xp.reshape(nb, t, D_MODEL), xs.reshape(ns, 1, D_MODEL),
            jnp.stack(kp_l), jnp.stack(vp_l), jnp.stack(srp_l), jnp.stack(sip_l),
            jnp.stack(ks_l), jnp.stack(vs_l), jnp.stack(srs_l), jnp.stack(sis_l))
```

```python
import functools
import math

import numpy as np
import jax
import jax.numpy as jnp
from jax import lax
from jax.experimental import pallas as pl
from jax.experimental.pallas import tpu as pltpu

F32 = jnp.float32
BF16 = jnp.bfloat16

D_MODEL = 2048
DEPTH = 4
HEAD_DIM = 64
N_HEADS = 16
N_KV = 2
GRP = N_HEADS // N_KV
WINDOW = 128
Q_WIDTH = N_HEADS * HEAD_DIM
KV_WIDTH = N_KV * HEAD_DIM
SSM_WIDTH = 1024
SSM_CG = 16
SSM_GROUPS = 64
SSM_STATE = 64
IN_WIDTH = Q_WIDTH + 2 * KV_WIDTH + SSM_WIDTH
D_FF = 5632
PLE_DIM = 256
EPS = 1e-6
NEG_INF = -1e30

LANES = 128
SUBLANES = 8

SLAB_GROUPS = 16
N_SLABS = SSM_GROUPS // SLAB_GROUPS
SLAB_CH = SLAB_GROUPS * SSM_CG
SLAB_ST = SLAB_GROUPS * SSM_STATE
ST_TILES = SLAB_ST // LANES
SSM_CHUNK = 256
SLABS_PER_STEP = 2

TM_RES = 512
TN_W = 256
TM_FFN = 1024
TF_FFN = 256
SAMPLE_PER_STEP = 8

VMEM_LIMIT = 56 * 1024 * 1024

_SLOPES = [float(v) for v in
           (2.0 ** (-8.0 * np.arange(1, N_HEADS + 1, dtype=np.float32) / N_HEADS)).astype(np.float32)]


def _rms(x, g):
    var = jnp.mean(x * x, axis=-1, keepdims=True)
    return x * lax.rsqrt(var + EPS) * g


def _iota2(shape):
    return (lax.broadcasted_iota(jnp.int32, shape, 0), lax.broadcasted_iota(jnp.int32, shape, 1))


def _gelu_tanh(x):
    c = math.sqrt(2.0 / math.pi)
    return 0.5 * x * (1.0 + jnp.tanh(c * (x + 0.044715 * (x * x * x))))


def _params(sem):
    return pltpu.CompilerParams(dimension_semantics=sem, vmem_limit_bytes=VMEM_LIMIT)


def _tile_index(s, nw, nt):
    return jnp.minimum(jnp.maximum(s - nw, 0), nt - 1)


def _load_weight_chunks(s, w_ref, wres_ref, nw):
    tn = w_ref.shape[1]
    for c in range(nw):
        @pl.when(s == c)
        def _(c=c):
            wres_ref[:, c * tn:(c + 1) * tn] = w_ref[...].astype(BF16)


def _inproj_kernel(x_ref, xs_ref, g_ref, w_ref, o_ref, os_ref, wres_ref, *, nw, nt):
    s = pl.program_id(0)
    _load_weight_chunks(s, w_ref, wres_ref, nw)

    @pl.when((s >= nw) & (s < nw + nt))
    def _():
        h = _rms(x_ref[...], g_ref[...]).astype(BF16)
        o_ref[...] = jnp.dot(h, wres_ref[...], preferred_element_type=F32)

    @pl.when(s == nw + nt)
    def _():
        h = _rms(xs_ref[...], g_ref[...]).astype(BF16)
        os_ref[...] = jnp.dot(h, wres_ref[...], preferred_element_type=F32)


def _inproj(l, x, xs, g, w):
    m, ns = x.shape[0], xs.shape[0]
    n = w.shape[2]
    nw, nt = n // TN_W, m // TM_RES
    tile = lambda s: (_tile_index(s, nw, nt), 0)
    const = lambda s: (0, 0)
    return pl.pallas_call(
        functools.partial(_inproj_kernel, nw=nw, nt=nt),
        grid=(nw + nt + 1,),
        in_specs=[
            pl.BlockSpec((TM_RES, D_MODEL), tile),
            pl.BlockSpec((ns, D_MODEL), const),
            pl.BlockSpec((None, 1, D_MODEL), lambda s: (l, 0, 0)),
            pl.BlockSpec((None, D_MODEL, TN_W), lambda s: (l, 0, jnp.minimum(s, nw - 1))),
        ],
        out_specs=[pl.BlockSpec((TM_RES, n), tile), pl.BlockSpec((ns, n), const)],
        out_shape=[jax.ShapeDtypeStruct((m, n), F32), jax.ShapeDtypeStruct((ns, n), F32)],
        scratch_shapes=[pltpu.VMEM((D_MODEL, n), BF16)],
        compiler_params=_params(("arbitrary",)),
        name="inproj",
    )(x, xs, g, w)


def _attn_prompt_kernel(sink_ref, q_ref, kp_ref, kc_ref, vp_ref, vc_ref, o_ref, bias_ref, *, l):
    blk = pl.program_id(1)

    @pl.when((pl.program_id(0) == 0) & (blk == 0))
    def _():
        qi, kj = _iota2((WINDOW, 2 * WINDOW))
        dist_i = qi - kj + WINDOW
        band = (dist_i >= 0) & (dist_i <= WINDOW)
        dist = dist_i.astype(F32)
        for h in range(N_HEADS):
            alibi = -(_SLOPES[h] * dist)
            bias_ref[0, h] = jnp.where(band & (kj >= WINDOW), alibi, NEG_INF)
            bias_ref[1, h] = jnp.where(band, alibi, NEG_INF)

    table = jnp.minimum(blk, 1)
    kcat = jnp.concatenate([kp_ref[...], kc_ref[...]], axis=0).astype(BF16)
    vcat = jnp.concatenate([vp_ref[...], vc_ref[...]], axis=0).astype(BF16)
    q = (q_ref[...] * (HEAD_DIM ** -0.5)).astype(BF16)
    for h in range(N_HEADS):
        kv = h // GRP
        qh = q[:, HEAD_DIM * h:HEAD_DIM * (h + 1)]
        kh = kcat[:, HEAD_DIM * kv:HEAD_DIM * (kv + 1)]
        vh = vcat[:, HEAD_DIM * kv:HEAD_DIM * (kv + 1)]
        s = lax.dot_general(qh, kh, (((1,), (1,)), ((), ())), preferred_element_type=F32)
        s = s + bias_ref[table, h]
        sink = sink_ref[l, h]
        m = jnp.maximum(jnp.max(s, axis=-1, keepdims=True), sink)
        e = jnp.exp(s - m)
        denom = jnp.sum(e, axis=-1, keepdims=True) + jnp.exp(sink - m)
        oh = jnp.dot(e.astype(BF16), vh, preferred_element_type=F32) / denom
        o_ref[:, HEAD_DIM * h:HEAD_DIM * (h + 1)] = oh


def _attn_prompt(l, z, sinks, nb, t):
    nblk = t // WINDOW
    kcol = Q_WIDTH // KV_WIDTH
    cur = lambda c: (lambda b, i: (b * nblk + i, c))
    prev = lambda c: (lambda b, i: (b * nblk + jnp.maximum(i - 1, 0), c))
    return pl.pallas_call(
        functools.partial(_attn_prompt_kernel, l=l),
        grid=(nb, nblk),
        in_specs=[
            pl.BlockSpec(memory_space=pltpu.SMEM),
            pl.BlockSpec((WINDOW, Q_WIDTH), cur(0)),
            pl.BlockSpec((WINDOW, KV_WIDTH), prev(kcol)),
            pl.BlockSpec((WINDOW, KV_WIDTH), cur(kcol)),
            pl.BlockSpec((WINDOW, KV_WIDTH), prev(kcol + 1)),
            pl.BlockSpec((WINDOW, KV_WIDTH), cur(kcol + 1)),
        ],
        out_specs=pl.BlockSpec((WINDOW, Q_WIDTH), cur(0)),
        out_shape=jax.ShapeDtypeStruct((nb * t, Q_WIDTH), F32),
        scratch_shapes=[pltpu.VMEM((2, N_HEADS, WINDOW, 2 * WINDOW), F32)],
        compiler_params=_params(("arbitrary", "arbitrary")),
        name="attn_prompt",
    )(sinks, z, z, z, z, z)


def _attn_sample_kernel(q_ref, kvn_ref, ck_ref, cv_ref, sink_ref, slope_ref, o_ref):
    dist = (WINDOW - lax.broadcasted_iota(jnp.int32, (GRP, WINDOW), 1)).astype(F32)
    for n in range(q_ref.shape[0]):
        ck = ck_ref[n].astype(BF16)
        cv = cv_ref[n].astype(BF16)
        kn = kvn_ref[n, :, :KV_WIDTH]
        vn = kvn_ref[n, :, KV_WIDTH:]
        for kv in range(N_KV):
            lo, hi = HEAD_DIM * kv, HEAD_DIM * (kv + 1)
            qk = q_ref[n, GRP * kv:GRP * (kv + 1), :]
            slope = slope_ref[GRP * kv:GRP * (kv + 1), :]
            sink = sink_ref[GRP * kv:GRP * (kv + 1), :]
            s_c = lax.dot_general(qk.astype(BF16), ck[:, lo:hi], (((1,), (1,)), ((), ())),
                                  preferred_element_type=F32) * (HEAD_DIM ** -0.5)
            s_c = s_c - slope * dist
            s_n = jnp.sum(qk * kn[:, lo:hi], axis=-1, keepdims=True) * (HEAD_DIM ** -0.5)
            m = jnp.maximum(jnp.maximum(jnp.max(s_c, axis=-1, keepdims=True), s_n), sink)
            e_c = jnp.exp(s_c - m)
            e_n = jnp.exp(s_n - m)
            denom = jnp.sum(e_c, axis=-1, keepdims=True) + e_n + jnp.exp(sink - m)
            o = jnp.dot(e_c.astype(BF16), cv[:, lo:hi], preferred_element_type=F32) + e_n * vn[:, lo:hi]
            o_ref[n, GRP * kv:GRP * (kv + 1), :] = o / denom


def _attn_sample(l, q3, kvn, ck, cv, sink_col, slope_col):
    n = q3.shape[0]
    sp = SAMPLE_PER_STEP
    return pl.pallas_call(
        _attn_sample_kernel,
        grid=(n // sp,),
        in_specs=[
            pl.BlockSpec((sp, N_HEADS, HEAD_DIM), lambda i: (i, 0, 0)),
            pl.BlockSpec((sp, 1, 2 * KV_WIDTH), lambda i: (i, 0, 0)),
            pl.BlockSpec((None, sp, WINDOW, KV_WIDTH), lambda i: (l, i, 0, 0)),
            pl.BlockSpec((None, sp, WINDOW, KV_WIDTH), lambda i: (l, i, 0, 0)),
            pl.BlockSpec((None, N_HEADS, 1), lambda i: (l, 0, 0)),
            pl.BlockSpec((N_HEADS, 1), lambda i: (0, 0)),
        ],
        out_specs=pl.BlockSpec((sp, N_HEADS, HEAD_DIM), lambda i: (i, 0, 0)),
        out_shape=jax.ShapeDtypeStruct((n, N_HEADS, HEAD_DIM), F32),
        compiler_params=_params(("arbitrary",)),
        name="attn_sample",
    )(q3, kvn, ck, cv, sink_col, slope_col)


def _discretise(a_re, a_im, log_dt):
    dt = jnp.exp(log_dt)
    dta_re = dt * a_re
    dta_im = dt * a_im
    mag = jnp.exp(dta_re)
    return mag * jnp.cos(dta_im), mag * jnp.sin(dta_im)


def _ssm_prep_kernel(are_ref, aim_ref, ldt_ref, bre_ref, bim_ref, cre_ref, cim_ref, wg_ref,
                     ared_ref, aimd_ref, ldtd_ref,
                     wb_ref, wbd_ref, wcr_ref, wci_ref, wglu_ref, ad_ref, ar_ref):
    copy_dot = lambda a, b: jnp.dot(a.astype(BF16), b, preferred_element_type=F32)

    a_re = are_ref[...]
    a_im = aim_ref[...]
    ab_re, ab_im = _discretise(a_re, a_im, ldt_ref[...])
    ar_ref[0] = ab_re
    ar_ref[1] = ab_im
    den = a_re * a_re + a_im * a_im
    f_re = ((ab_re - 1.0) * a_re + ab_im * a_im) / den
    f_im = (ab_im * a_re - (ab_re - 1.0) * a_im) / den
    r, _ = _iota2((LANES, SLAB_ST))
    f_t = jnp.where(r == 0, f_re, jnp.where(r == 1, f_im, 0.0)).T
    f_re = f_t[:, 0:1]
    f_im = f_t[:, 1:2]

    b_re = bre_ref[...]
    b_im = bim_ref[...]
    r, c = _iota2((SSM_CG, SLAB_CH))
    tile_c = (r == (c & (SSM_CG - 1))).astype(BF16)
    r, c = _iota2((SLAB_ST, SLAB_CH))
    diag = (r >> 6) == (c >> 4)
    bbt_re = jnp.where(diag, copy_dot(f_re * b_re - f_im * b_im, tile_c), 0.0)
    bbt_im = jnp.where(diag, copy_dot(f_re * b_im + f_im * b_re, tile_c), 0.0)
    wb_re = bbt_re.T
    wb_im = bbt_im.T
    wb_ref[:, :SLAB_ST] = wb_re.astype(BF16)
    wb_ref[:, SLAB_ST:] = wb_im.astype(BF16)
    r, _ = _iota2((SLAB_CH, LANES))
    own = [(r >> 5) == k for k in range(ST_TILES)]
    fold = lambda w: sum(jnp.where(own[k], w[:, LANES * k:LANES * (k + 1)], 0.0) for k in range(ST_TILES))
    wbd_ref[:, :LANES] = fold(wb_re).astype(BF16)
    wbd_ref[:, LANES:] = fold(wb_im).astype(BF16)

    r, c = _iota2((SSM_STATE, SLAB_ST))
    tile_p = (r == (c & (SSM_STATE - 1))).astype(BF16)
    r, c = _iota2((SLAB_CH, SLAB_ST))
    diag = (r >> 4) == (c >> 6)
    wcr_ref[...] = jnp.where(diag, copy_dot(cre_ref[...], tile_p), 0.0).T.astype(BF16)
    wci_ref[...] = jnp.where(diag, copy_dot(cim_ref[...], tile_p), 0.0).T.astype(BF16)

    r, c = _iota2((2 * SSM_CG, 2 * SLAB_CH))
    tile_e = (r == (c & (SSM_CG - 1)) + SSM_CG * (c >> 8)).astype(BF16)
    r, c = _iota2((SLAB_CH, 2 * SLAB_CH))
    diag = (r >> 4) == ((c & (SLAB_CH - 1)) >> 4)
    wglu_ref[...] = jnp.where(diag, copy_dot(wg_ref[...], tile_e), 0.0).astype(BF16)

    abd_re, abd_im = _discretise(ared_ref[...], aimd_ref[...], ldtd_ref[...])
    ad_ref[0] = abd_re
    ad_ref[1] = abd_im


def _ssm_prep(a_re, a_im, log_dt, b_re, b_im, c_re, c_im, w_glu):
    gp = SSM_GROUPS * SSM_STATE
    gc = SSM_GROUPS * SSM_CG
    ldt_gp = jnp.repeat(log_dt, SSM_STATE, axis=1)
    row = lambda a: a.reshape(DEPTH, N_SLABS, 1, SLAB_ST)
    dense = lambda a: a.reshape(DEPTH, N_SLABS, SUBLANES, LANES)
    blk = lambda rows, cols: pl.BlockSpec((None, rows, cols), lambda l, s: (l, s, 0))
    rblk = pl.BlockSpec((None, None, 1, SLAB_ST), lambda l, s: (l, s, 0, 0))
    dblk = pl.BlockSpec((None, None, SUBLANES, LANES), lambda l, s: (l, s, 0, 0))
    oblk = lambda rows, cols: pl.BlockSpec((None, None, rows, cols), lambda l, s: (l, s, 0, 0))
    return pl.pallas_call(
        _ssm_prep_kernel,
        grid=(DEPTH, N_SLABS),
        in_specs=[rblk, rblk, rblk,
                  blk(SLAB_ST, SSM_CG), blk(SLAB_ST, SSM_CG),
                  blk(SLAB_CH, SSM_STATE), blk(SLAB_CH, SSM_STATE), blk(SLAB_CH, 2 * SSM_CG),
                  dblk, dblk, dblk],
        out_specs=[oblk(SLAB_CH, 2 * SLAB_ST), oblk(SLAB_CH, 2 * LANES),
                   oblk(SLAB_ST, SLAB_CH), oblk(SLAB_ST, SLAB_CH), oblk(SLAB_CH, 2 * SLAB_CH),
                   pl.BlockSpec((None, None, 2, SUBLANES, LANES), lambda l, s: (l, s, 0, 0, 0)),
                   pl.BlockSpec((None, None, 2, 1, SLAB_ST), lambda l, s: (l, s, 0, 0, 0))],
        out_shape=[jax.ShapeDtypeStruct((DEPTH, N_SLABS, SLAB_CH, 2 * SLAB_ST), BF16),
                   jax.ShapeDtypeStruct((DEPTH, N_SLABS, SLAB_CH, 2 * LANES), BF16),
                   jax.ShapeDtypeStruct((DEPTH, N_SLABS, SLAB_ST, SLAB_CH), BF16),
                   jax.ShapeDtypeStruct((DEPTH, N_SLABS, SLAB_ST, SLAB_CH), BF16),
                   jax.ShapeDtypeStruct((DEPTH, N_SLABS, SLAB_CH, 2 * SLAB_CH), BF16),
                   jax.ShapeDtypeStruct((DEPTH, N_SLABS, 2, SUBLANES, LANES), F32),
                   jax.ShapeDtypeStruct((DEPTH, N_SLABS, 2, 1, SLAB_ST), F32)],
        compiler_params=_params(("arbitrary", "arbitrary")),
        name="ssm_prep",
    )(row(a_re), row(a_im), row(ldt_gp),
      b_re.reshape(DEPTH, gp, SSM_CG), b_im.reshape(DEPTH, gp, SSM_CG),
      c_re.reshape(DEPTH, gc, SSM_STATE), c_im.reshape(DEPTH, gc, SSM_STATE),
      w_glu.reshape(DEPTH, gc, 2 * SSM_CG),
      dense(a_re), dense(a_im), dense(ldt_gp))


def _ssm_tail(s_re, s_im, u, wcr, wci, d, wglu):
    y = (jnp.dot(s_re.astype(BF16), wcr, preferred_element_type=F32)
         - jnp.dot(s_im.astype(BF16), wci, preferred_element_type=F32)
         + d * u)
    z = jnp.dot(_gelu_tanh(y).astype(BF16), wglu, preferred_element_type=F32)
    return z[:, :SLAB_CH] * jax.nn.sigmoid(z[:, SLAB_CH:])


def _ssm_prompt_kernel(*refs):
    u_refs = refs[:SLABS_PER_STEP]
    (wbd_ref, a_ref, wcr_ref, wci_ref, d_ref, wglu_ref,
     o_ref, sre_ref, sim_ref, st_ref, carry_ref) = refs[SLABS_PER_STEP:]
    nb = u_refs[0].shape[0]
    chains = [(p, b) for p in range(SLABS_PER_STEP) for b in range(nb)]

    @pl.when(pl.program_id(1) == 0)
    def _():
        carry_ref[...] = jnp.zeros_like(carry_ref)

    sub, ch = _iota2((SUBLANES, SLAB_CH))
    own = sub == (ch >> 5)
    for p, b in chains:
        lhs = jnp.where(own[None], u_refs[p][b][:, None, :], 0.0).astype(BF16)
        bu = jnp.dot(lhs.reshape(SUBLANES * SSM_CHUNK, SLAB_CH), wbd_ref[p],
                     preferred_element_type=F32)
        st_ref[p, b, 0] = bu[:, :LANES]
        st_ref[p, b, 1] = bu[:, LANES:]

    abar = [(a_ref[p, 0], a_ref[p, 1]) for p in range(SLABS_PER_STEP)]

    def step(t, carry):
        row = pl.multiple_of(t * SUBLANES, SUBLANES)
        new = []
        for i, (p, b) in enumerate(chains):
            a_re, a_im = abar[p]
            sr, si = carry[2 * i], carry[2 * i + 1]
            nr = a_re * sr - a_im * si + st_ref[p, b, 0, pl.ds(row, SUBLANES), :]
            ni = a_re * si + a_im * sr + st_ref[p, b, 1, pl.ds(row, SUBLANES), :]
            st_ref[p, b, 0, pl.ds(row, SUBLANES), :] = nr
            st_ref[p, b, 1, pl.ds(row, SUBLANES), :] = ni
            new += [nr, ni]
        return tuple(new)

    init = tuple(carry_ref[p, b, j] for p, b in chains for j in range(2))
    fin = lax.fori_loop(0, SSM_CHUNK, step, init, unroll=2)
    for i, (p, b) in enumerate(chains):
        carry_ref[p, b, 0] = fin[2 * i]
        carry_ref[p, b, 1] = fin[2 * i + 1]
        sre_ref[p, b] = fin[2 * i]
        sim_ref[p, b] = fin[2 * i + 1]

    tiles = lambda p, b, j: jnp.concatenate(
        [st_ref[p, b, j, pl.ds(k, SSM_CHUNK, stride=SUBLANES), :] for k in range(ST_TILES)], axis=1)
    for p, b in chains:
        o_ref[b, :, SLAB_CH * p:SLAB_CH * (p + 1)] = _ssm_tail(
            tiles(p, b, 0), tiles(p, b, 1), u_refs[p][b], wcr_ref[p], wci_ref[p], d_ref[p], wglu_ref[p])


def _slab_specs(l):
    w4 = lambda rows, cols: pl.BlockSpec((None, None, rows, cols), lambda s, *_: (l, s, 0, 0))
    return dict(wb=w4(SLAB_CH, 2 * SLAB_ST), wbd=w4(SLAB_CH, 2 * LANES), wc=w4(SLAB_ST, SLAB_CH),
                d=w4(1, SLAB_CH), wglu=w4(SLAB_CH, 2 * SLAB_CH))


def _ssm_prompt(l, z3, wbd, a_dense, wcr, wci, d_row, wglu):
    nb, t, _ = z3.shape
    sps = SLABS_PER_STEP
    ucol = (Q_WIDTH + 2 * KV_WIDTH) // SLAB_CH
    u_spec = lambda p: pl.BlockSpec((nb, SSM_CHUNK, SLAB_CH), lambda s, c: (0, c, ucol + sps * s + p))
    w4 = lambda rows, cols: pl.BlockSpec((None, sps, rows, cols), lambda s, c: (l, s, 0, 0))
    st_blk = pl.BlockSpec((sps, nb, SUBLANES, LANES), lambda s, c: (s, 0, 0, 0))
    return pl.pallas_call(
        _ssm_prompt_kernel,
        grid=(N_SLABS // sps, t // SSM_CHUNK),
        in_specs=[u_spec(p) for p in range(sps)] + [
            w4(SLAB_CH, 2 * LANES),
            pl.BlockSpec((None, sps, 2, SUBLANES, LANES), lambda s, c: (l, s, 0, 0, 0)),
            w4(SLAB_ST, SLAB_CH), w4(SLAB_ST, SLAB_CH), w4(1, SLAB_CH), w4(SLAB_CH, 2 * SLAB_CH),
        ],
        out_specs=[pl.BlockSpec((nb, SSM_CHUNK, sps * SLAB_CH), lambda s, c: (0, c, s)), st_blk, st_blk],
        out_shape=[
            jax.ShapeDtypeStruct((nb, t, SSM_WIDTH), F32),
            jax.ShapeDtypeStruct((N_SLABS, nb, SUBLANES, LANES), F32),
            jax.ShapeDtypeStruct((N_SLABS, nb, SUBLANES, LANES), F32),
        ],
        scratch_shapes=[
            pltpu.VMEM((sps, nb, 2, SUBLANES * SSM_CHUNK, LANES), F32),
            pltpu.VMEM((sps, nb, 2, SUBLANES, LANES), F32),
        ],
        compiler_params=_params(("arbitrary", "arbitrary")),
        name="ssm_prompt",
    )(*([z3] * sps), wbd, a_dense, wcr, wci, d_row, wglu)


def _ssm_sample_kernel(u_ref, wb_ref, a_ref, s0r_ref, s0i_ref, wcr_ref, wci_ref, d_ref, wglu_ref,
                       o_ref, sre_ref, sim_ref):
    u = u_ref[...]
    bu = jnp.dot(u.astype(BF16), wb_ref[...], preferred_element_type=F32)
    a_re = a_ref[0]
    a_im = a_ref[1]
    s0r = s0r_ref[...]
    s0i = s0i_ref[...]
    nr = bu[:, :SLAB_ST] + a_re * s0r - a_im * s0i
    ni = bu[:, SLAB_ST:] + a_re * s0i + a_im * s0r
    sre_ref[...] = nr
    sim_ref[...] = ni
    o_ref[...] = _ssm_tail(nr, ni, u, wcr_ref[...], wci_ref[...], d_ref[...], wglu_ref[...])


def _ssm_sample(l, z, wb, a_row, s0r, s0i, wcr, wci, d_row, wglu):
    n = z.shape[0]
    ucol = (Q_WIDTH + 2 * KV_WIDTH) // SLAB_CH
    sp = _slab_specs(l)
    st_in = pl.BlockSpec((None, n, SLAB_ST), lambda s: (l, 0, s))
    st_out = pl.BlockSpec((n, SLAB_ST), lambda s: (0, s))
    return pl.pallas_call(
        _ssm_sample_kernel,
        grid=(N_SLABS,),
        in_specs=[
            pl.BlockSpec((n, SLAB_CH), lambda s: (0, ucol + s)),
            sp["wb"],
            pl.BlockSpec((None, None, 2, 1, SLAB_ST), lambda s: (l, s, 0, 0, 0)),
            st_in, st_in, sp["wc"], sp["wc"], sp["d"], sp["wglu"],
        ],
        out_specs=[pl.BlockSpec((n, SLAB_CH), lambda s: (0, s)), st_out, st_out],
        out_shape=[
            jax.ShapeDtypeStruct((n, SSM_WIDTH), F32),
            jax.ShapeDtypeStruct((n, SSM_GROUPS * SSM_STATE), F32),
            jax.ShapeDtypeStruct((n, SSM_GROUPS * SSM_STATE), F32),
        ],
        compiler_params=_params(("arbitrary",)),
        name="ssm_sample",
    )(z, wb, a_row, s0r, s0i, wcr, wci, d_row, wglu)


def _outproj_kernel(a_ref, m_ref, x_ref, as_ref, ms_ref, xs_ref, ga_ref, gs_ref, gp_ref, w_ref,
                    o_ref, os_ref, wres_ref, *, nw, nt):
    s = pl.program_id(0)
    _load_weight_chunks(s, w_ref, wres_ref, nw)

    def tile(attn, ssm, x):
        merged = jnp.concatenate([_rms(attn, ga_ref[...]).astype(BF16),
                                  _rms(ssm, gs_ref[...]).astype(BF16)], axis=1)
        y = jnp.dot(merged, wres_ref[...], preferred_element_type=F32)
        return x + _rms(y, gp_ref[...])

    @pl.when((s >= nw) & (s < nw + nt))
    def _():
        o_ref[...] = tile(a_ref[...], m_ref[...], x_ref[...])

    @pl.when(s == nw + nt)
    def _():
        os_ref[...] = tile(as_ref[...], ms_ref[...], xs_ref[...])


def _outproj(l, attn, ssm, x, attn_s, ssm_s, xs, ga, gs, gp, w):
    m, ns = x.shape[0], xs.shape[0]
    nw, nt = D_MODEL // TN_W, m // TM_RES
    tile = lambda s: (_tile_index(s, nw, nt), 0)
    const = lambda s: (0, 0)
    gain = lambda n: pl.BlockSpec((None, 1, n), lambda s: (l, 0, 0))
    return pl.pallas_call(
        functools.partial(_outproj_kernel, nw=nw, nt=nt),
        grid=(nw + nt + 1,),
        in_specs=[
            pl.BlockSpec((TM_RES, Q_WIDTH), tile),
            pl.BlockSpec((TM_RES, SSM_WIDTH), tile),
            pl.BlockSpec((TM_RES, D_MODEL), tile),
            pl.BlockSpec((ns, Q_WIDTH), const),
            pl.BlockSpec((ns, SSM_WIDTH), const),
            pl.BlockSpec((ns, D_MODEL), const),
            gain(Q_WIDTH), gain(SSM_WIDTH), gain(D_MODEL),
            pl.BlockSpec((None, D_MODEL, TN_W), lambda s: (l, 0, jnp.minimum(s, nw - 1))),
        ],
        out_specs=[pl.BlockSpec((TM_RES, D_MODEL), tile), pl.BlockSpec((ns, D_MODEL), const)],
        out_shape=[jax.ShapeDtypeStruct((m, D_MODEL), F32), jax.ShapeDtypeStruct((ns, D_MODEL), F32)],
        scratch_shapes=[pltpu.VMEM((D_MODEL, D_MODEL), BF16)],
        compiler_params=_params(("arbitrary",)),
        name="outproj",
    )(attn, ssm, x, attn_s, ssm_s, xs, ga, gs, gp, w)


def _ffn_kernel(x_ref, xs_ref, g1_ref, wg_ref, wu_ref, wd_ref, g2_ref, o_ref, os_ref, h_ref):
    i = pl.program_id(0)
    j = pl.program_id(1)
    nj = pl.num_programs(1)
    tm = x_ref.shape[0]

    @pl.when(j == 0)
    def _():
        h_ref[:tm, :] = _rms(x_ref[...], g1_ref[...]).astype(BF16)
        h_ref[tm:, :] = _rms(xs_ref[...], g1_ref[...]).astype(BF16)
        o_ref[...] = jnp.zeros_like(o_ref)

    @pl.when((j == 0) & (i == 0))
    def _():
        os_ref[...] = jnp.zeros_like(os_ref)

    h = h_ref[...]
    gate = jnp.dot(h, wg_ref[...].astype(BF16), preferred_element_type=F32)
    up = jnp.dot(h, wu_ref[...].astype(BF16), preferred_element_type=F32)
    act = (gate * jax.nn.sigmoid(gate) * up).astype(BF16)
    down = jnp.dot(act, wd_ref[...].astype(BF16), preferred_element_type=F32)
    o_ref[...] += down[:tm]

    @pl.when(i == 0)
    def _():
        os_ref[...] += down[tm:]

    @pl.when(j == nj - 1)
    def _():
        o_ref[...] = x_ref[...] + _rms(o_ref[...], g2_ref[...])

    @pl.when((j == nj - 1) & (i == 0))
    def _():
        os_ref[...] = xs_ref[...] + _rms(os_ref[...], g2_ref[...])


def _ffn(l, x, xs, g1, wgu, wd, g2):
    m, ns = x.shape[0], xs.shape[0]
    nf = D_FF // TF_FFN
    gain = pl.BlockSpec((None, 1, D_MODEL), lambda i, j: (l, 0, 0))
    return pl.pallas_call(
        _ffn_kernel,
        grid=(m // TM_FFN, nf),
        in_specs=[
            pl.BlockSpec((TM_FFN, D_MODEL), lambda i, j: (i, 0), pipeline_mode=pl.Buffered(1)),
            pl.BlockSpec((ns, D_MODEL), lambda i, j: (0, 0)),
            gain,
            pl.BlockSpec((None, D_MODEL, TF_FFN), lambda i, j: (l, 0, j)),
            pl.BlockSpec((None, D_MODEL, TF_FFN), lambda i, j: (l, 0, nf + j)),
            pl.BlockSpec((None, TF_FFN, D_MODEL), lambda i, j: (l, j, 0)),
            gain,
        ],
        out_specs=[pl.BlockSpec((TM_FFN, D_MODEL), lambda i, j: (i, 0)),
                   pl.BlockSpec((ns, D_MODEL), lambda i, j: (0, 0))],
        out_shape=[jax.ShapeDtypeStruct((m, D_MODEL), F32), jax.ShapeDtypeStruct((ns, D_MODEL), F32)],
        scratch_shapes=[pltpu.VMEM((TM_FFN + ns, D_MODEL), BF16)],
        compiler_params=_params(("arbitrary", "arbitrary")),
        name="ffn",
    )(x, xs, g1, wgu, wgu, wd, g2)


def _ple_kernel(x_ref, pe_ref, xs_ref, pes_ref, wg_ref, wp_ref, o_ref, os_ref, wgres_ref, wpres_ref,
                *, nw, nt):
    s = pl.program_id(0)
    _load_weight_chunks(s, wg_ref, wgres_ref, nw)

    @pl.when(s == 0)
    def _():
        wpres_ref[...] = wp_ref[...].astype(BF16)

    def tile(x, pe):
        gate = jnp.dot(x.astype(BF16), wgres_ref[...], preferred_element_type=F32)
        proj = jnp.dot(pe.astype(BF16), wpres_ref[...], preferred_element_type=F32)
        return x + jax.nn.sigmoid(gate) * proj

    @pl.when((s >= nw) & (s < nw + nt))
    def _():
        o_ref[...] = tile(x_ref[...], pe_ref[...])

    @pl.when(s == nw + nt)
    def _():
        os_ref[...] = tile(xs_ref[...], pes_ref[...])


def _ple(l, x, pe, xs, pes, wg, wp):
    m, ns = x.shape[0], xs.shape[0]
    nw, nt = D_MODEL // TN_W, m // TM_RES
    tile = lambda s: (_tile_index(s, nw, nt), 0)
    const = lambda s: (0, 0)
    return pl.pallas_call(
        functools.partial(_ple_kernel, nw=nw, nt=nt),
        grid=(nw + nt + 1,),
        in_specs=[
            pl.BlockSpec((TM_RES, D_MODEL), tile),
            pl.BlockSpec((None, TM_RES, PLE_DIM), lambda s: (l, _tile_index(s, nw, nt), 0)),
            pl.BlockSpec((ns, D_MODEL), const),
            pl.BlockSpec((None, ns, PLE_DIM), lambda s: (l, 0, 0)),
            pl.BlockSpec((None, D_MODEL, TN_W), lambda s: (l, 0, jnp.minimum(s, nw - 1))),
            pl.BlockSpec((None, PLE_DIM, D_MODEL), lambda s: (l, 0, 0)),
        ],
        out_specs=[pl.BlockSpec((TM_RES, D_MODEL), tile), pl.BlockSpec((ns, D_MODEL), const)],
        out_shape=[jax.ShapeDtypeStruct((m, D_MODEL), F32), jax.ShapeDtypeStruct((ns, D_MODEL), F32)],
        scratch_shapes=[pltpu.VMEM((D_MODEL, D_MODEL), BF16), pltpu.VMEM((PLE_DIM, D_MODEL), BF16)],
        compiler_params=_params(("arbitrary",)),
        name="ple",
    )(x, pe, xs, pes, wg, wp)


def kernel(x_prompt, x_sample, cache_k, cache_v, state_ssm_re, state_ssm_im, p_prompt, p_sample,
           g_pre_mix, w_in, attn_sinks, ssm_a_re, ssm_a_im, ssm_log_dt, ssm_b_re, ssm_b_im,
           ssm_c_re, ssm_c_im, ssm_d, ssm_w_glu, g_attn_out, g_ssm_out, w_out, g_post_mix,
           g_pre_ffn, w_gate_up, w_down, g_post_ffn, w_ple_gate, w_ple_proj):
    nb, t, _ = x_prompt.shape
    ns = x_sample.shape[0]

    wb, wbd, wcr, wci, wglu, a_dense, a_row = _ssm_prep(ssm_a_re, ssm_a_im, ssm_log_dt, ssm_b_re, ssm_b_im,
                                                        ssm_c_re, ssm_c_im, ssm_w_glu)
    d_row = ssm_d.reshape(DEPTH, N_SLABS, 1, SLAB_CH)
    slope_col = jnp.asarray(_SLOPES, F32).reshape(N_HEADS, 1)
    sink_col = attn_sinks.reshape(DEPTH, N_HEADS, 1)
    gain = lambda g: g.reshape(DEPTH, 1, g.shape[-1])
    g_pre_mix, g_attn_out, g_ssm_out, g_post_mix, g_pre_ffn, g_post_ffn = map(
        gain, (g_pre_mix, g_attn_out, g_ssm_out, g_post_mix, g_pre_ffn, g_post_ffn))

    xp = x_prompt.reshape(nb * t, D_MODEL)
    xs = x_sample.reshape(ns, D_MODEL)
    pp = p_prompt.reshape(DEPTH, nb * t, PLE_DIM)
    ps = p_sample.reshape(DEPTH, ns, PLE_DIM)
    s0r = state_ssm_re.reshape(DEPTH, ns, SSM_GROUPS * SSM_STATE)
    s0i = state_ssm_im.reshape(DEPTH, ns, SSM_GROUPS * SSM_STATE)
    ck = cache_k.reshape(DEPTH, ns, WINDOW, KV_WIDTH)
    cv = cache_v.reshape(DEPTH, ns, WINDOW, KV_WIDTH)

    kv_lo, kv_hi = Q_WIDTH, Q_WIDTH + 2 * KV_WIDTH

    kvp_l, srp_l, sip_l, kvs_l, srs_l, sis_l = [], [], [], [], [], []
    for l in range(DEPTH):
        z, zs = _inproj(l, xp, xs, g_pre_mix, w_in)
        z3 = z.reshape(nb, t, IN_WIDTH)

        attn = _attn_prompt(l, z, attn_sinks, nb, t)
        ssm, sre, sim = _ssm_prompt(l, z3, wbd, a_dense, wcr, wci, d_row, wglu)

        kvn = zs[:, kv_lo:kv_hi].reshape(ns, 1, 2 * KV_WIDTH)
        attn_s = _attn_sample(l, zs[:, :Q_WIDTH].reshape(ns, N_HEADS, HEAD_DIM), kvn,
                              ck, cv, sink_col, slope_col)
        ssm_s, srs, sis = _ssm_sample(l, zs, wb, a_row, s0r, s0i, wcr, wci, d_row, wglu)

        x1, x1s = _outproj(l, attn, ssm.reshape(nb * t, SSM_WIDTH), xp,
                           attn_s.reshape(ns, Q_WIDTH), ssm_s, xs,
                           g_attn_out, g_ssm_out, g_post_mix, w_out)
        x2, x2s = _ffn(l, x1, x1s, g_pre_ffn, w_gate_up, w_down, g_post_ffn)
        xp, xs = _ple(l, x2, pp, x2s, ps, w_ple_gate, w_ple_proj)

        kvp_l.append(z3[:, t - WINDOW:, kv_lo:kv_hi])
        kvs_l.append(kvn)
        srp_l.append(sre)
        sip_l.append(sim)
        srs_l.append(srs)
        sis_l.append(sis)

    heads = lambda a: a.reshape(a.shape[:-1] + (N_KV, HEAD_DIM))
    kvp = jnp.stack(kvp_l)
    kvs = jnp.stack(kvs_l)
    new_ks = jnp.concatenate([ck[:, :, 1:], kvs[..., :KV_WIDTH]], axis=2)
    new_vs = jnp.concatenate([cv[:, :, 1:], kvs[..., KV_WIDTH:]], axis=2)
    unslab = lambda s: jnp.swapaxes(jnp.stack(s), 1, 2).reshape(DEPTH, nb, SSM_GROUPS, SSM_STATE)
    states = lambda s: jnp.stack(s).reshape(DEPTH, ns, SSM_GROUPS, SSM_STATE)
    return (xp.reshape(nb, t, D_MODEL), xs.reshape(ns, 1, D_MODEL),
            heads(kvp[..., :KV_WIDTH]), heads(kvp[..., KV_WIDTH:]), unslab(srp_l), unslab(sip_l),
            heads(new_ks), heads(new_vs), states(srs_l), states(sis_l))
```

```python
import functools
import math

import numpy as np
import jax
import jax.numpy as jnp
from jax import lax
from jax.experimental import pallas as pl
from jax.experimental.pallas import tpu as pltpu

F32 = jnp.float32
BF16 = jnp.bfloat16

D_MODEL = 2048
DEPTH = 4
HEAD_DIM = 64
N_HEADS = 16
N_KV = 2
GRP = N_HEADS // N_KV
WINDOW = 128
Q_WIDTH = N_HEADS * HEAD_DIM
KV_WIDTH = N_KV * HEAD_DIM
SSM_WIDTH = 1024
SSM_CG = 16
SSM_GROUPS = 64
SSM_STATE = 64
IN_WIDTH = Q_WIDTH + 2 * KV_WIDTH + SSM_WIDTH
D_FF = 5632
PLE_DIM = 256
EPS = 1e-6
NEG_INF = -1e30

LANES = 128
SUBLANES = 8

SLAB_GROUPS = 16
N_SLABS = SSM_GROUPS // SLAB_GROUPS
SLAB_CH = SLAB_GROUPS * SSM_CG
SLAB_ST = SLAB_GROUPS * SSM_STATE
ST_TILES = SLAB_ST // LANES
SSM_CHUNK = 256
SLABS_PER_STEP = 2

TM_RES = 512
TN_W = 256
TM_FFN = 1024
TF_FFN = 256
SAMPLE_PER_STEP = 8
ATTN_ROWS = 128

VMEM_LIMIT = 56 * 1024 * 1024

_SLOPES = [float(v) for v in
           (2.0 ** (-8.0 * np.arange(1, N_HEADS + 1, dtype=np.float32) / N_HEADS)).astype(np.float32)]


def _rms(x, g):
    var = jnp.mean(x * x, axis=-1, keepdims=True)
    return x * lax.rsqrt(var + EPS) * g


def _iota2(shape):
    return (lax.broadcasted_iota(jnp.int32, shape, 0), lax.broadcasted_iota(jnp.int32, shape, 1))


def _gelu_tanh(x):
    c = math.sqrt(2.0 / math.pi)
    return 0.5 * x * (1.0 + jnp.tanh(c * (x + 0.044715 * (x * x * x))))


def _params(sem):
    return pltpu.CompilerParams(dimension_semantics=sem, vmem_limit_bytes=VMEM_LIMIT)


def _tile_index(s, nw, nt):
    return jnp.minimum(jnp.maximum(s - nw + 1, 0), nt - 1)


def _is_tile_step(s, nw, nt):
    return (s >= nw) & (s < nw + nt - 1)


def _on_weight_chunks(s, w_ref, wres_ref, nw, apply):
    tn = w_ref.shape[1]
    for c in range(nw):
        @pl.when(s == c)
        def _(c=c):
            chunk = w_ref[...].astype(BF16)
            wres_ref[:, c * tn:(c + 1) * tn] = chunk
            apply(slice(c * tn, (c + 1) * tn), chunk)


def _inproj_kernel(x_ref, xs_ref, g_ref, w_ref, o_ref, os_ref, wres_ref, h0_ref, *, nw, nt):
    s = pl.program_id(0)

    @pl.when(s == 0)
    def _():
        h0_ref[...] = _rms(x_ref[...], g_ref[...]).astype(BF16)

    def tile0(cols, chunk):
        o_ref[:, cols] = jnp.dot(h0_ref[...], chunk, preferred_element_type=F32)

    _on_weight_chunks(s, w_ref, wres_ref, nw, tile0)

    @pl.when(_is_tile_step(s, nw, nt))
    def _():
        h = _rms(x_ref[...], g_ref[...]).astype(BF16)
        o_ref[...] = jnp.dot(h, wres_ref[...], preferred_element_type=F32)

    @pl.when(s == nw + nt - 1)
    def _():
        h = _rms(xs_ref[...], g_ref[...]).astype(BF16)
        os_ref[...] = jnp.dot(h, wres_ref[...], preferred_element_type=F32)


def _inproj(l, x, xs, g, w):
    m, ns = x.shape[0], xs.shape[0]
    n = w.shape[2]
    nw, nt = n // TN_W, m // TM_RES
    tile = lambda s: (_tile_index(s, nw, nt), 0)
    const = lambda s: (0, 0)
    return pl.pallas_call(
        functools.partial(_inproj_kernel, nw=nw, nt=nt),
        grid=(nw + nt,),
        in_specs=[
            pl.BlockSpec((TM_RES, D_MODEL), tile),
            pl.BlockSpec((ns, D_MODEL), const),
            pl.BlockSpec((None, 1, D_MODEL), lambda s: (l, 0, 0)),
            pl.BlockSpec((None, D_MODEL, TN_W), lambda s: (l, 0, jnp.minimum(s, nw - 1))),
        ],
        out_specs=[pl.BlockSpec((TM_RES, n), tile), pl.BlockSpec((ns, n), const)],
        out_shape=[jax.ShapeDtypeStruct((m, n), F32), jax.ShapeDtypeStruct((ns, n), F32)],
        scratch_shapes=[pltpu.VMEM((D_MODEL, n), BF16), pltpu.VMEM((TM_RES, D_MODEL), BF16)],
        compiler_params=_params(("arbitrary",)),
        name="inproj",
    )(x, xs, g, w)


def _attn_prompt_kernel(sink_ref, q_ref, kp_ref, kc_ref, vp_ref, vc_ref, o_ref, bias_ref, *, l):
    blk = pl.program_id(1)

    @pl.when((pl.program_id(0) == 0) & (blk == 0))
    def _():
        qi, kj = _iota2((WINDOW, 2 * WINDOW))
        dist_i = qi - kj + WINDOW
        band = (dist_i >= 0) & (dist_i <= WINDOW)
        dist = dist_i.astype(F32)
        for h in range(N_HEADS):
            alibi = -(_SLOPES[h] * dist)
            bias_ref[0, h] = jnp.where(band & (kj >= WINDOW), alibi, NEG_INF)
            bias_ref[1, h] = jnp.where(band, alibi, NEG_INF)

    table = jnp.minimum(blk, 1)
    kcat = jnp.concatenate([kp_ref[...], kc_ref[...]], axis=0).astype(BF16)
    vcat = jnp.concatenate([vp_ref[...], vc_ref[...]], axis=0).astype(BF16)
    q = (q_ref[...] * (HEAD_DIM ** -0.5)).astype(BF16)
    for h in range(N_HEADS):
        kv = h // GRP
        kh = kcat[:, HEAD_DIM * kv:HEAD_DIM * (kv + 1)]
        vh = vcat[:, HEAD_DIM * kv:HEAD_DIM * (kv + 1)]
        sink = sink_ref[l, h]
        for r0 in range(0, WINDOW, ATTN_ROWS):
            rows = slice(r0, r0 + ATTN_ROWS)
            qh = q[rows, HEAD_DIM * h:HEAD_DIM * (h + 1)]
            s = lax.dot_general(qh, kh, (((1,), (1,)), ((), ())), preferred_element_type=F32)
            s = s + bias_ref[table, h, rows, :]
            m = jnp.maximum(jnp.max(s, axis=-1, keepdims=True), sink)
            e = jnp.exp(s - m)
            denom = jnp.sum(e, axis=-1, keepdims=True) + jnp.exp(sink - m)
            oh = jnp.dot(e.astype(BF16), vh, preferred_element_type=F32) / denom
            o_ref[rows, HEAD_DIM * h:HEAD_DIM * (h + 1)] = oh


def _attn_prompt(l, z, sinks, nb, t):
    nblk = t // WINDOW
    kcol = Q_WIDTH // KV_WIDTH
    cur = lambda c: (lambda b, i: (b * nblk + i, c))
    prev = lambda c: (lambda b, i: (b * nblk + jnp.maximum(i - 1, 0), c))
    return pl.pallas_call(
        functools.partial(_attn_prompt_kernel, l=l),
        grid=(nb, nblk),
        in_specs=[
            pl.BlockSpec(memory_space=pltpu.SMEM),
            pl.BlockSpec((WINDOW, Q_WIDTH), cur(0)),
            pl.BlockSpec((WINDOW, KV_WIDTH), prev(kcol)),
            pl.BlockSpec((WINDOW, KV_WIDTH), cur(kcol)),
            pl.BlockSpec((WINDOW, KV_WIDTH), prev(kcol + 1)),
            pl.BlockSpec((WINDOW, KV_WIDTH), cur(kcol + 1)),
        ],
        out_specs=pl.BlockSpec((WINDOW, Q_WIDTH), cur(0)),
        out_shape=jax.ShapeDtypeStruct((nb * t, Q_WIDTH), F32),
        scratch_shapes=[pltpu.VMEM((2, N_HEADS, WINDOW, 2 * WINDOW), F32)],
        compiler_params=_params(("arbitrary", "arbitrary")),
        name="attn_prompt",
    )(sinks, z, z, z, z, z)


def _attn_sample_kernel(q_ref, kvn_ref, ck_ref, cv_ref, sink_ref, slope_ref, o_ref):
    dist = (WINDOW - lax.broadcasted_iota(jnp.int32, (GRP, WINDOW), 1)).astype(F32)
    for n in range(q_ref.shape[0]):
        ck = ck_ref[n].astype(BF16)
        cv = cv_ref[n].astype(BF16)
        kn = kvn_ref[n, :, :KV_WIDTH]
        vn = kvn_ref[n, :, KV_WIDTH:]
        for kv in range(N_KV):
            lo, hi = HEAD_DIM * kv, HEAD_DIM * (kv + 1)
            qk = q_ref[n, GRP * kv:GRP * (kv + 1), :]
            slope = slope_ref[GRP * kv:GRP * (kv + 1), :]
            sink = sink_ref[GRP * kv:GRP * (kv + 1), :]
            s_c = lax.dot_general(qk.astype(BF16), ck[:, lo:hi], (((1,), (1,)), ((), ())),
                                  preferred_element_type=F32) * (HEAD_DIM ** -0.5)
            s_c = s_c - slope * dist
            s_n = jnp.sum(qk * kn[:, lo:hi], axis=-1, keepdims=True) * (HEAD_DIM ** -0.5)
            m = jnp.maximum(jnp.maximum(jnp.max(s_c, axis=-1, keepdims=True), s_n), sink)
            e_c = jnp.exp(s_c - m)
            e_n = jnp.exp(s_n - m)
            denom = jnp.sum(e_c, axis=-1, keepdims=True) + e_n + jnp.exp(sink - m)
            o = jnp.dot(e_c.astype(BF16), cv[:, lo:hi], preferred_element_type=F32) + e_n * vn[:, lo:hi]
            o_ref[n, GRP * kv:GRP * (kv + 1), :] = o / denom


def _attn_sample(l, q3, kvn, ck, cv, sink_col, slope_col):
    n = q3.shape[0]
    sp = SAMPLE_PER_STEP
    return pl.pallas_call(
        _attn_sample_kernel,
        grid=(n // sp,),
        in_specs=[
            pl.BlockSpec((sp, N_HEADS, HEAD_DIM), lambda i: (i, 0, 0)),
            pl.BlockSpec((sp, 1, 2 * KV_WIDTH), lambda i: (i, 0, 0)),
            pl.BlockSpec((None, sp, WINDOW, KV_WIDTH), lambda i: (l, i, 0, 0)),
            pl.BlockSpec((None, sp, WINDOW, KV_WIDTH), lambda i: (l, i, 0, 0)),
            pl.BlockSpec((None, N_HEADS, 1), lambda i: (l, 0, 0)),
            pl.BlockSpec((N_HEADS, 1), lambda i: (0, 0)),
        ],
        out_specs=pl.BlockSpec((sp, N_HEADS, HEAD_DIM), lambda i: (i, 0, 0)),
        out_shape=jax.ShapeDtypeStruct((n, N_HEADS, HEAD_DIM), F32),
        compiler_params=_params(("arbitrary",)),
        name="attn_sample",
    )(q3, kvn, ck, cv, sink_col, slope_col)


def _discretise(a_re, a_im, log_dt):
    dt = jnp.exp(log_dt)
    dta_re = dt * a_re
    dta_im = dt * a_im
    mag = jnp.exp(dta_re)
    return mag * jnp.cos(dta_im), mag * jnp.sin(dta_im)


def _ssm_prep_kernel(are_ref, aim_ref, ldt_ref, bre_ref, bim_ref, cre_ref, cim_ref, wg_ref,
                     ared_ref, aimd_ref, ldtd_ref,
                     wb_ref, wbd_ref, wcr_ref, wci_ref, wglu_ref, ad_ref, ar_ref):
    copy_dot = lambda a, b: jnp.dot(a.astype(BF16), b, preferred_element_type=F32)

    a_re = are_ref[...]
    a_im = aim_ref[...]
    ab_re, ab_im = _discretise(a_re, a_im, ldt_ref[...])
    ar_ref[0] = ab_re
    ar_ref[1] = ab_im
    den = a_re * a_re + a_im * a_im
    f_re = ((ab_re - 1.0) * a_re + ab_im * a_im) / den
    f_im = (ab_im * a_re - (ab_re - 1.0) * a_im) / den
    r, _ = _iota2((LANES, SLAB_ST))
    f_t = jnp.where(r == 0, f_re, jnp.where(r == 1, f_im, 0.0)).T
    f_re = f_t[:, 0:1]
    f_im = f_t[:, 1:2]

    b_re = bre_ref[...]
    b_im = bim_ref[...]
    r, c = _iota2((SSM_CG, SLAB_CH))
    tile_c = (r == (c & (SSM_CG - 1))).astype(BF16)
    r, c = _iota2((SLAB_ST, SLAB_CH))
    diag = (r >> 6) == (c >> 4)
    bbt_re = jnp.where(diag, copy_dot(f_re * b_re - f_im * b_im, tile_c), 0.0)
    bbt_im = jnp.where(diag, copy_dot(f_re * b_im + f_im * b_re, tile_c), 0.0)
    wb_re = bbt_re.T
    wb_im = bbt_im.T
    wb_ref[:, :SLAB_ST] = wb_re.astype(BF16)
    wb_ref[:, SLAB_ST:] = wb_im.astype(BF16)
    r, _ = _iota2((SLAB_CH, LANES))
    own = [(r >> 5) == k for k in range(ST_TILES)]
    fold = lambda w: sum(jnp.where(own[k], w[:, LANES * k:LANES * (k + 1)], 0.0) for k in range(ST_TILES))
    wbd_ref[:, :LANES] = fold(wb_re).astype(BF16)
    wbd_ref[:, LANES:] = fold(wb_im).astype(BF16)

    r, c = _iota2((SSM_STATE, SLAB_ST))
    tile_p = (r == (c & (SSM_STATE - 1))).astype(BF16)
    r, c = _iota2((SLAB_CH, SLAB_ST))
    diag = (r >> 4) == (c >> 6)
    wcr_ref[...] = jnp.where(diag, copy_dot(cre_ref[...], tile_p), 0.0).T.astype(BF16)
    wci_ref[...] = jnp.where(diag, copy_dot(cim_ref[...], tile_p), 0.0).T.astype(BF16)

    r, c = _iota2((2 * SSM_CG, 2 * SLAB_CH))
    tile_e = (r == (c & (SSM_CG - 1)) + SSM_CG * (c >> 8)).astype(BF16)
    r, c = _iota2((SLAB_CH, 2 * SLAB_CH))
    diag = (r >> 4) == ((c & (SLAB_CH - 1)) >> 4)
    wglu_ref[...] = jnp.where(diag, copy_dot(wg_ref[...], tile_e), 0.0).astype(BF16)

    abd_re, abd_im = _discretise(ared_ref[...], aimd_ref[...], ldtd_ref[...])
    ad_ref[0] = abd_re
    ad_ref[1] = abd_im


def _ssm_prep(a_re, a_im, log_dt, b_re, b_im, c_re, c_im, w_glu):
    gp = SSM_GROUPS * SSM_STATE
    gc = SSM_GROUPS * SSM_CG
    ldt_gp = jnp.repeat(log_dt, SSM_STATE, axis=1)
    row = lambda a: a.reshape(DEPTH, N_SLABS, 1, SLAB_ST)
    dense = lambda a: a.reshape(DEPTH, N_SLABS, SUBLANES, LANES)
    blk = lambda rows, cols: pl.BlockSpec((None, rows, cols), lambda l, s: (l, s, 0))
    rblk = pl.BlockSpec((None, None, 1, SLAB_ST), lambda l, s: (l, s, 0, 0))
    dblk = pl.BlockSpec((None, None, SUBLANES, LANES), lambda l, s: (l, s, 0, 0))
    oblk = lambda rows, cols: pl.BlockSpec((None, None, rows, cols), lambda l, s: (l, s, 0, 0))
    return pl.pallas_call(
        _ssm_prep_kernel,
        grid=(DEPTH, N_SLABS),
        in_specs=[rblk, rblk, rblk,
                  blk(SLAB_ST, SSM_CG), blk(SLAB_ST, SSM_CG),
                  blk(SLAB_CH, SSM_STATE), blk(SLAB_CH, SSM_STATE), blk(SLAB_CH, 2 * SSM_CG),
                  dblk, dblk, dblk],
        out_specs=[oblk(SLAB_CH, 2 * SLAB_ST), oblk(SLAB_CH, 2 * LANES),
                   oblk(SLAB_ST, SLAB_CH), oblk(SLAB_ST, SLAB_CH), oblk(SLAB_CH, 2 * SLAB_CH),
                   pl.BlockSpec((None, None, 2, SUBLANES, LANES), lambda l, s: (l, s, 0, 0, 0)),
                   pl.BlockSpec((None, None, 2, 1, SLAB_ST), lambda l, s: (l, s, 0, 0, 0))],
        out_shape=[jax.ShapeDtypeStruct((DEPTH, N_SLABS, SLAB_CH, 2 * SLAB_ST), BF16),
                   jax.ShapeDtypeStruct((DEPTH, N_SLABS, SLAB_CH, 2 * LANES), BF16),
                   jax.ShapeDtypeStruct((DEPTH, N_SLABS, SLAB_ST, SLAB_CH), BF16),
                   jax.ShapeDtypeStruct((DEPTH, N_SLABS, SLAB_ST, SLAB_CH), BF16),
                   jax.ShapeDtypeStruct((DEPTH, N_SLABS, SLAB_CH, 2 * SLAB_CH), BF16),
                   jax.ShapeDtypeStruct((DEPTH, N_SLABS, 2, SUBLANES, LANES), F32),
                   jax.ShapeDtypeStruct((DEPTH, N_SLABS, 2, 1, SLAB_ST), F32)],
        compiler_params=_params(("arbitrary", "arbitrary")),
        name="ssm_prep",
    )(row(a_re), row(a_im), row(ldt_gp),
      b_re.reshape(DEPTH, gp, SSM_CG), b_im.reshape(DEPTH, gp, SSM_CG),
      c_re.reshape(DEPTH, gc, SSM_STATE), c_im.reshape(DEPTH, gc, SSM_STATE),
      w_glu.reshape(DEPTH, gc, 2 * SSM_CG),
      dense(a_re), dense(a_im), dense(ldt_gp))


def _ssm_tail(s_re, s_im, u, wcr, wci, d, wglu):
    y = (jnp.dot(s_re.astype(BF16), wcr, preferred_element_type=F32)
         - jnp.dot(s_im.astype(BF16), wci, preferred_element_type=F32)
         + d * u)
    z = jnp.dot(_gelu_tanh(y).astype(BF16), wglu, preferred_element_type=F32)
    return z[:, :SLAB_CH] * jax.nn.sigmoid(z[:, SLAB_CH:])


def _ssm_prompt_kernel(*refs):
    u_refs = refs[:SLABS_PER_STEP]
    (wbd_ref, a_ref, wcr_ref, wci_ref, d_ref, wglu_ref,
     o_ref, sre_ref, sim_ref, st_ref, carry_ref) = refs[SLABS_PER_STEP:]
    nb = u_refs[0].shape[0]
    chains = [(p, b) for p in range(SLABS_PER_STEP) for b in range(nb)]

    @pl.when(pl.program_id(1) == 0)
    def _():
        carry_ref[...] = jnp.zeros_like(carry_ref)

    sub, ch = _iota2((SUBLANES, SLAB_CH))
    own = sub == (ch >> 5)
    for p, b in chains:
        lhs = jnp.where(own[None], u_refs[p][b][:, None, :], 0.0).astype(BF16)
        bu = jnp.dot(lhs.reshape(SUBLANES * SSM_CHUNK, SLAB_CH), wbd_ref[p],
                     preferred_element_type=F32)
        st_ref[p, b, 0] = bu[:, :LANES]
        st_ref[p, b, 1] = bu[:, LANES:]

    abar = [(a_ref[p, 0], a_ref[p, 1]) for p in range(SLABS_PER_STEP)]

    def step(t, carry):
        row = pl.multiple_of(t * SUBLANES, SUBLANES)
        new = []
        for i, (p, b) in enumerate(chains):
            a_re, a_im = abar[p]
            sr, si = carry[2 * i], carry[2 * i + 1]
            nr = a_re * sr - a_im * si + st_ref[p, b, 0, pl.ds(row, SUBLANES), :]
            ni = a_re * si + a_im * sr + st_ref[p, b, 1, pl.ds(row, SUBLANES), :]
            st_ref[p, b, 0, pl.ds(row, SUBLANES), :] = nr
            st_ref[p, b, 1, pl.ds(row, SUBLANES), :] = ni
            new += [nr, ni]
        return tuple(new)

    init = tuple(carry_ref[p, b, j] for p, b in chains for j in range(2))
    fin = lax.fori_loop(0, SSM_CHUNK, step, init, unroll=2)
    for i, (p, b) in enumerate(chains):
        carry_ref[p, b, 0] = fin[2 * i]
        carry_ref[p, b, 1] = fin[2 * i + 1]
        sre_ref[p, b] = fin[2 * i]
        sim_ref[p, b] = fin[2 * i + 1]

    tiles = lambda p, b, j: jnp.concatenate(
        [st_ref[p, b, j, pl.ds(k, SSM_CHUNK, stride=SUBLANES), :] for k in range(ST_TILES)], axis=1)
    for p, b in chains:
        o_ref[b, :, SLAB_CH * p:SLAB_CH * (p + 1)] = _ssm_tail(
            tiles(p, b, 0), tiles(p, b, 1), u_refs[p][b], wcr_ref[p], wci_ref[p], d_ref[p], wglu_ref[p])


def _slab_specs(l):
    w4 = lambda rows, cols: pl.BlockSpec((None, None, rows, cols), lambda s, *_: (l, s, 0, 0))
    return dict(wb=w4(SLAB_CH, 2 * SLAB_ST), wc=w4(SLAB_ST, SLAB_CH),
                d=w4(1, SLAB_CH), wglu=w4(SLAB_CH, 2 * SLAB_CH))


def _ssm_prompt(l, z3, wbd, a_dense, wcr, wci, d_row, wglu):
    nb, t, _ = z3.shape
    sps = SLABS_PER_STEP
    ucol = (Q_WIDTH + 2 * KV_WIDTH) // SLAB_CH
    u_spec = lambda p: pl.BlockSpec((nb, SSM_CHUNK, SLAB_CH), lambda s, c: (0, c, ucol + sps * s + p))
    w4 = lambda rows, cols: pl.BlockSpec((None, sps, rows, cols), lambda s, c: (l, s, 0, 0))
    st_blk = pl.BlockSpec((sps, nb, SUBLANES, LANES), lambda s, c: (s, 0, 0, 0))
    return pl.pallas_call(
        _ssm_prompt_kernel,
        grid=(N_SLABS // sps, t // SSM_CHUNK),
        in_specs=[u_spec(p) for p in range(sps)] + [
            w4(SLAB_CH, 2 * LANES),
            pl.BlockSpec((None, sps, 2, SUBLANES, LANES), lambda s, c: (l, s, 0, 0, 0)),
            w4(SLAB_ST, SLAB_CH), w4(SLAB_ST, SLAB_CH), w4(1, SLAB_CH), w4(SLAB_CH, 2 * SLAB_CH),
        ],
        out_specs=[pl.BlockSpec((nb, SSM_CHUNK, sps * SLAB_CH), lambda s, c: (0, c, s)), st_blk, st_blk],
        out_shape=[
            jax.ShapeDtypeStruct((nb, t, SSM_WIDTH), F32),
            jax.ShapeDtypeStruct((N_SLABS, nb, SUBLANES, LANES), F32),
            jax.ShapeDtypeStruct((N_SLABS, nb, SUBLANES, LANES), F32),
        ],
        scratch_shapes=[
            pltpu.VMEM((sps, nb, 2, SUBLANES * SSM_CHUNK, LANES), F32),
            pltpu.VMEM((sps, nb, 2, SUBLANES, LANES), F32),
        ],
        compiler_params=_params(("arbitrary", "arbitrary")),
        name="ssm_prompt",
    )(*([z3] * sps), wbd, a_dense, wcr, wci, d_row, wglu)


def _ssm_sample_kernel(u_ref, wb_ref, a_ref, s0r_ref, s0i_ref, wcr_ref, wci_ref, d_ref, wglu_ref,
                       o_ref, sre_ref, sim_ref):
    u = u_ref[...]
    bu = jnp.dot(u.astype(BF16), wb_ref[...], preferred_element_type=F32)
    a_re = a_ref[0]
    a_im = a_ref[1]
    s0r = s0r_ref[...]
    s0i = s0i_ref[...]
    nr = bu[:, :SLAB_ST] + a_re * s0r - a_im * s0i
    ni = bu[:, SLAB_ST:] + a_re * s0i + a_im * s0r
    sre_ref[...] = nr
    sim_ref[...] = ni
    o_ref[...] = _ssm_tail(nr, ni, u, wcr_ref[...], wci_ref[...], d_ref[...], wglu_ref[...])


def _ssm_sample(l, z, wb, a_row, s0r, s0i, wcr, wci, d_row, wglu):
    n = z.shape[0]
    ucol = (Q_WIDTH + 2 * KV_WIDTH) // SLAB_CH
    sp = _slab_specs(l)
    st_in = pl.BlockSpec((None, n, SLAB_ST), lambda s: (l, 0, s))
    st_out = pl.BlockSpec((n, SLAB_ST), lambda s: (0, s))
    return pl.pallas_call(
        _ssm_sample_kernel,
        grid=(N_SLABS,),
        in_specs=[
            pl.BlockSpec((n, SLAB_CH), lambda s: (0, ucol + s)),
            sp["wb"],
            pl.BlockSpec((None, None, 2, 1, SLAB_ST), lambda s: (l, s, 0, 0, 0)),
            st_in, st_in, sp["wc"], sp["wc"], sp["d"], sp["wglu"],
        ],
        out_specs=[pl.BlockSpec((n, SLAB_CH), lambda s: (0, s)), st_out, st_out],
        out_shape=[
            jax.ShapeDtypeStruct((n, SSM_WIDTH), F32),
            jax.ShapeDtypeStruct((n, SSM_GROUPS * SSM_STATE), F32),
            jax.ShapeDtypeStruct((n, SSM_GROUPS * SSM_STATE), F32),
        ],
        compiler_params=_params(("arbitrary",)),
        name="ssm_sample",
    )(z, wb, a_row, s0r, s0i, wcr, wci, d_row, wglu)


def _outproj_kernel(a_ref, m_ref, x_ref, as_ref, ms_ref, xs_ref, ga_ref, gs_ref, gp_ref, w_ref,
                    o_ref, os_ref, wres_ref, m0_ref, y0_ref, *, nw, nt):
    s = pl.program_id(0)

    def merge(attn, ssm):
        return jnp.concatenate([_rms(attn, ga_ref[...]).astype(BF16),
                                _rms(ssm, gs_ref[...]).astype(BF16)], axis=1)

    def tile(attn, ssm, x):
        y = jnp.dot(merge(attn, ssm), wres_ref[...], preferred_element_type=F32)
        return x + _rms(y, gp_ref[...])

    @pl.when(s == 0)
    def _():
        m0_ref[...] = merge(a_ref[...], m_ref[...])

    def tile0(cols, chunk):
        y0_ref[:, cols] = jnp.dot(m0_ref[...], chunk, preferred_element_type=F32)

    _on_weight_chunks(s, w_ref, wres_ref, nw, tile0)

    @pl.when(s == nw - 1)
    def _():
        o_ref[...] = x_ref[...] + _rms(y0_ref[...], gp_ref[...])

    @pl.when(_is_tile_step(s, nw, nt))
    def _():
        o_ref[...] = tile(a_ref[...], m_ref[...], x_ref[...])

    @pl.when(s == nw + nt - 1)
    def _():
        os_ref[...] = tile(as_ref[...], ms_ref[...], xs_ref[...])


def _outproj(l, attn, ssm, x, attn_s, ssm_s, xs, ga, gs, gp, w):
    m, ns = x.shape[0], xs.shape[0]
    nw, nt = D_MODEL // TN_W, m // TM_RES
    tile = lambda s: (_tile_index(s, nw, nt), 0)
    const = lambda s: (0, 0)
    gain = lambda n: pl.BlockSpec((None, 1, n), lambda s: (l, 0, 0))
    return pl.pallas_call(
        functools.partial(_outproj_kernel, nw=nw, nt=nt),
        grid=(nw + nt,),
        in_specs=[
            pl.BlockSpec((TM_RES, Q_WIDTH), tile),
            pl.BlockSpec((TM_RES, SSM_WIDTH), tile),
            pl.BlockSpec((TM_RES, D_MODEL), tile),
            pl.BlockSpec((ns, Q_WIDTH), const),
            pl.BlockSpec((ns, SSM_WIDTH), const),
            pl.BlockSpec((ns, D_MODEL), const),
            gain(Q_WIDTH), gain(SSM_WIDTH), gain(D_MODEL),
            pl.BlockSpec((None, D_MODEL, TN_W), lambda s: (l, 0, jnp.minimum(s, nw - 1))),
        ],
        out_specs=[pl.BlockSpec((TM_RES, D_MODEL), tile), pl.BlockSpec((ns, D_MODEL), const)],
        out_shape=[jax.ShapeDtypeStruct((m, D_MODEL), F32), jax.ShapeDtypeStruct((ns, D_MODEL), F32)],
        scratch_shapes=[pltpu.VMEM((D_MODEL, D_MODEL), BF16), pltpu.VMEM((TM_RES, D_MODEL), BF16),
                        pltpu.VMEM((TM_RES, D_MODEL), F32)],
        compiler_params=_params(("arbitrary",)),
        name="outproj",
    )(attn, ssm, x, attn_s, ssm_s, xs, ga, gs, gp, w)


def _ffn_kernel(x_ref, xs_ref, g1_ref, wg_ref, wu_ref, wd_ref, g2_ref, o_ref, os_ref, h_ref):
    i = pl.program_id(0)
    j = pl.program_id(1)
    nj = pl.num_programs(1)
    tm = x_ref.shape[0]

    @pl.when(j == 0)
    def _():
        h_ref[:tm, :] = _rms(x_ref[...], g1_ref[...]).astype(BF16)
        o_ref[...] = jnp.zeros_like(o_ref)

    @pl.when((j == 0) & (i == 0))
    def _():
        h_ref[tm:, :] = _rms(xs_ref[...], g1_ref[...]).astype(BF16)
        os_ref[...] = jnp.zeros_like(os_ref)

    def swiglu_down(h):
        gate = jnp.dot(h, wg_ref[...].astype(BF16), preferred_element_type=F32)
        up = jnp.dot(h, wu_ref[...].astype(BF16), preferred_element_type=F32)
        act = (gate * jax.nn.sigmoid(gate) * up).astype(BF16)
        return jnp.dot(act, wd_ref[...].astype(BF16), preferred_element_type=F32)

    @pl.when(i == 0)
    def _():
        down = swiglu_down(h_ref[...])
        o_ref[...] += down[:tm]
        os_ref[...] += down[tm:]

    @pl.when(i > 0)
    def _():
        o_ref[...] += swiglu_down(h_ref[:tm, :])

    @pl.when(j == nj - 1)
    def _():
        o_ref[...] = x_ref[...] + _rms(o_ref[...], g2_ref[...])

    @pl.when((j == nj - 1) & (i == 0))
    def _():
        os_ref[...] = xs_ref[...] + _rms(os_ref[...], g2_ref[...])


def _ffn(l, x, xs, g1, wgu, wd, g2):
    m, ns = x.shape[0], xs.shape[0]
    nf = D_FF // TF_FFN
    gain = pl.BlockSpec((None, 1, D_MODEL), lambda i, j: (l, 0, 0))
    return pl.pallas_call(
        _ffn_kernel,
        grid=(m // TM_FFN, nf),
        in_specs=[
            pl.BlockSpec((TM_FFN, D_MODEL), lambda i, j: (i, 0), pipeline_mode=pl.Buffered(1)),
            pl.BlockSpec((ns, D_MODEL), lambda i, j: (0, 0)),
            gain,
            pl.BlockSpec((None, D_MODEL, TF_FFN), lambda i, j: (l, 0, j)),
            pl.BlockSpec((None, D_MODEL, TF_FFN), lambda i, j: (l, 0, nf + j)),
            pl.BlockSpec((None, TF_FFN, D_MODEL), lambda i, j: (l, j, 0)),
            gain,
        ],
        out_specs=[pl.BlockSpec((TM_FFN, D_MODEL), lambda i, j: (i, 0)),
                   pl.BlockSpec((ns, D_MODEL), lambda i, j: (0, 0))],
        out_shape=[jax.ShapeDtypeStruct((m, D_MODEL), F32), jax.ShapeDtypeStruct((ns, D_MODEL), F32)],
        scratch_shapes=[pltpu.VMEM((TM_FFN + ns, D_MODEL), BF16)],
        compiler_params=_params(("arbitrary", "arbitrary")),
        name="ffn",
    )(x, xs, g1, wgu, wgu, wd, g2)


def _ple_kernel(x_ref, pe_ref, xs_ref, pes_ref, wg_ref, wp_ref, o_ref, os_ref,
                wgres_ref, wpres_ref, x0_ref, pe0_ref, *, nw, nt):
    s = pl.program_id(0)

    def tile(x, pe):
        gate = jnp.dot(x.astype(BF16), wgres_ref[...], preferred_element_type=F32)
        proj = jnp.dot(pe.astype(BF16), wpres_ref[...], preferred_element_type=F32)
        return x + jax.nn.sigmoid(gate) * proj

    @pl.when(s == 0)
    def _():
        x0_ref[...] = x_ref[...].astype(BF16)
        pe0_ref[...] = pe_ref[...].astype(BF16)

    def tile0(cols, gate_chunk):
        proj_chunk = wp_ref[...].astype(BF16)
        wpres_ref[:, cols] = proj_chunk
        gate = jnp.dot(x0_ref[...], gate_chunk, preferred_element_type=F32)
        proj = jnp.dot(pe0_ref[...], proj_chunk, preferred_element_type=F32)
        o_ref[:, cols] = x_ref[:, cols] + jax.nn.sigmoid(gate) * proj

    _on_weight_chunks(s, wg_ref, wgres_ref, nw, tile0)

    @pl.when(_is_tile_step(s, nw, nt))
    def _():
        o_ref[...] = tile(x_ref[...], pe_ref[...])

    @pl.when(s == nw + nt - 1)
    def _():
        os_ref[...] = tile(xs_ref[...], pes_ref[...])


def _ple(l, x, pe, xs, pes, wg, wp):
    m, ns = x.shape[0], xs.shape[0]
    nw, nt = D_MODEL // TN_W, m // TM_RES
    tile = lambda s: (_tile_index(s, nw, nt), 0)
    const = lambda s: (0, 0)
    chunk = lambda s: (l, 0, jnp.minimum(s, nw - 1))
    return pl.pallas_call(
        functools.partial(_ple_kernel, nw=nw, nt=nt),
        grid=(nw + nt,),
        in_specs=[
            pl.BlockSpec((TM_RES, D_MODEL), tile),
            pl.BlockSpec((None, TM_RES, PLE_DIM), lambda s: (l, _tile_index(s, nw, nt), 0)),
            pl.BlockSpec((ns, D_MODEL), const),
            pl.BlockSpec((None, ns, PLE_DIM), lambda s: (l, 0, 0)),
            pl.BlockSpec((None, D_MODEL, TN_W), chunk),
            pl.BlockSpec((None, PLE_DIM, TN_W), chunk),
        ],
        out_specs=[pl.BlockSpec((TM_RES, D_MODEL), tile), pl.BlockSpec((ns, D_MODEL), const)],
        out_shape=[jax.ShapeDtypeStruct((m, D_MODEL), F32), jax.ShapeDtypeStruct((ns, D_MODEL), F32)],
        scratch_shapes=[pltpu.VMEM((D_MODEL, D_MODEL), BF16), pltpu.VMEM((PLE_DIM, D_MODEL), BF16),
                        pltpu.VMEM((TM_RES, D_MODEL), BF16), pltpu.VMEM((TM_RES, PLE_DIM), BF16)],
        compiler_params=_params(("arbitrary",)),
        name="ple",
    )(x, pe, xs, pes, wg, wp)


def kernel(x_prompt, x_sample, cache_k, cache_v, state_ssm_re, state_ssm_im, p_prompt, p_sample,
           g_pre_mix, w_in, attn_sinks, ssm_a_re, ssm_a_im, ssm_log_dt, ssm_b_re, ssm_b_im,
           ssm_c_re, ssm_c_im, ssm_d, ssm_w_glu, g_attn_out, g_ssm_out, w_out, g_post_mix,
           g_pre_ffn, w_gate_up, w_down, g_post_ffn, w_ple_gate, w_ple_proj):
    nb, t, _ = x_prompt.shape
    ns = x_sample.shape[0]

    wb, wbd, wcr, wci, wglu, a_dense, a_row = _ssm_prep(ssm_a_re, ssm_a_im, ssm_log_dt, ssm_b_re, ssm_b_im,
                                                        ssm_c_re, ssm_c_im, ssm_w_glu)
    d_row = ssm_d.reshape(DEPTH, N_SLABS, 1, SLAB_CH)
    slope_col = jnp.asarray(_SLOPES, F32).reshape(N_HEADS, 1)
    sink_col = attn_sinks.reshape(DEPTH, N_HEADS, 1)
    gain = lambda g: g.reshape(DEPTH, 1, g.shape[-1])
    g_pre_mix, g_attn_out, g_ssm_out, g_post_mix, g_pre_ffn, g_post_ffn = map(
        gain, (g_pre_mix, g_attn_out, g_ssm_out, g_post_mix, g_pre_ffn, g_post_ffn))

    xp = x_prompt.reshape(nb * t, D_MODEL)
    xs = x_sample.reshape(ns, D_MODEL)
    pp = p_prompt.reshape(DEPTH, nb * t, PLE_DIM)
    ps = p_sample.reshape(DEPTH, ns, PLE_DIM)
    s0r = state_ssm_re.reshape(DEPTH, ns, SSM_GROUPS * SSM_STATE)
    s0i = state_ssm_im.reshape(DEPTH, ns, SSM_GROUPS * SSM_STATE)
    ck = cache_k.reshape(DEPTH, ns, WINDOW, KV_WIDTH)
    cv = cache_v.reshape(DEPTH, ns, WINDOW, KV_WIDTH)

    kv_lo, kv_hi = Q_WIDTH, Q_WIDTH + 2 * KV_WIDTH

    kvp_l, srp_l, sip_l, kvs_l, srs_l, sis_l = [], [], [], [], [], []
    for l in range(DEPTH):
        z, zs = _inproj(l, xp, xs, g_pre_mix, w_in)
        z3 = z.reshape(nb, t, IN_WIDTH)

        attn = _attn_prompt(l, z, attn_sinks, nb, t)
        ssm, sre, sim = _ssm_prompt(l, z3, wbd, a_dense, wcr, wci, d_row, wglu)

        kvn = zs[:, kv_lo:kv_hi].reshape(ns, 1, 2 * KV_WIDTH)
        attn_s = _attn_sample(l, zs[:, :Q_WIDTH].reshape(ns, N_HEADS, HEAD_DIM), kvn,
                              ck, cv, sink_col, slope_col)
        ssm_s, srs, sis = _ssm_sample(l, zs, wb, a_row, s0r, s0i, wcr, wci, d_row, wglu)

        x1, x1s = _outproj(l, attn, ssm.reshape(nb * t, SSM_WIDTH), xp,
                           attn_s.reshape(ns, Q_WIDTH), ssm_s, xs,
                           g_attn_out, g_ssm_out, g_post_mix, w_out)
        x2, x2s = _ffn(l, x1, x1s, g_pre_ffn, w_gate_up, w_down, g_post_ffn)
        xp, xs = _ple(l, x2, pp, x2s, ps, w_ple_gate, w_ple_proj)

        kvp_l.append(z3[:, t - WINDOW:, kv_lo:kv_hi])
        kvs_l.append(kvn)
        srp_l.append(sre)
        sip_l.append(sim)
        srs_l.append(srs)
        sis_l.append(sis)

    heads = lambda a: a.reshape(a.shape[:-1] + (N_KV, HEAD_DIM))
    kvp = jnp.stack(kvp_l)
    kvs = jnp.stack(kvs_l)
    new_ks = jnp.concatenate([ck[:, :, 1:], kvs[..., :KV_WIDTH]], axis=2)
    new_vs = jnp.concatenate([cv[:, :, 1:], kvs[..., KV_WIDTH:]], axis=2)
    unslab = lambda s: jnp.swapaxes(jnp.stack(s), 1, 2).reshape(DEPTH, nb, SSM_GROUPS, SSM_STATE)
    states = lambda s: jnp.stack(s).reshape(DEPTH, ns, SSM_GROUPS, SSM_STATE)
    return (xp.reshape(nb, t, D_MODEL), xs.reshape(ns, 1, D_MODEL),
            heads(kvp[..., :KV_WIDTH]), heads(kvp[..., KV_WIDTH:]), unslab(srp_l), unslab(sip_l),
            heads(new_ks), heads(new_vs), states(srs_l), states(sis_l))
```

```python
import functools
import math

import numpy as np
import jax
import jax.numpy as jnp
from jax import lax
from jax.experimental import pallas as pl
from jax.experimental.pallas import tpu as pltpu

F32 = jnp.float32
BF16 = jnp.bfloat16

D_MODEL = 2048
DEPTH = 4
HEAD_DIM = 64
N_HEADS = 16
N_KV = 2
GRP = N_HEADS // N_KV
WINDOW = 128
Q_WIDTH = N_HEADS * HEAD_DIM
KV_WIDTH = N_KV * HEAD_DIM
SSM_WIDTH = 1024
SSM_CG = 16
SSM_GROUPS = 64
SSM_STATE = 64
IN_WIDTH = Q_WIDTH + 2 * KV_WIDTH + SSM_WIDTH
D_FF = 5632
PLE_DIM = 256
EPS = 1e-6
NEG_INF = -1e30

LANES = 128
SUBLANES = 8

SLAB_GROUPS = 16
N_SLABS = SSM_GROUPS // SLAB_GROUPS
SLAB_CH = SLAB_GROUPS * SSM_CG
SLAB_ST = SLAB_GROUPS * SSM_STATE
ST_TILES = SLAB_ST // LANES
SSM_CHUNK = 256
SLABS_PER_STEP = 2

TM_RES = 512
TN_W = 256
TM_FFN = 1024
TF_FFN = 256
SAMPLE_PER_STEP = 8

VMEM_LIMIT = 56 * 1024 * 1024
VMEM_LIMIT_FUSED = 60 * 1024 * 1024

_SLOPES = [float(v) for v in
           (2.0 ** (-8.0 * np.arange(1, N_HEADS + 1, dtype=np.float32) / N_HEADS)).astype(np.float32)]


def _rms(x, g):
    var = jnp.mean(x * x, axis=-1, keepdims=True)
    return x * lax.rsqrt(var + EPS) * g


def _iota2(shape):
    return (lax.broadcasted_iota(jnp.int32, shape, 0), lax.broadcasted_iota(jnp.int32, shape, 1))


def _gelu_tanh(x):
    c = math.sqrt(2.0 / math.pi)
    return 0.5 * x * (1.0 + jnp.tanh(c * (x + 0.044715 * (x * x * x))))


def _params(sem, vmem_limit=VMEM_LIMIT):
    return pltpu.CompilerParams(dimension_semantics=sem, vmem_limit_bytes=vmem_limit)


def _tile_index(s, nw, nt):
    return jnp.minimum(jnp.maximum(s - nw + 1, 0), nt - 1)


def _is_tile_step(s, nw, nt):
    return (s >= nw) & (s < nw + nt - 1)


def _on_weight_chunks(s, w_ref, wres_ref, nw, apply):
    tn = w_ref.shape[1]
    for c in range(nw):
        @pl.when(s == c)
        def _(c=c):
            chunk = w_ref[...].astype(BF16)
            wres_ref[:, c * tn:(c + 1) * tn] = chunk
            apply(slice(c * tn, (c + 1) * tn), chunk)


def _attend_tile(tau, l, tiles_per_seq, sink_ref, bias_ref, zq_ref, zkv_ref, o_ref):
    tau = jnp.asarray(tau, jnp.int32)
    qslot = lax.rem(tau, 2)
    cur = lax.rem(tau, 3)
    prev = lax.rem(tau + 2, 3)
    seq_start = lax.rem(tau, tiles_per_seq) == 0
    lane = lax.broadcasted_iota(jnp.int32, (1, 2 * WINDOW), 1)
    hide_prev = jnp.where(seq_start & (lane < WINDOW), NEG_INF, 0.0)
    for bq in range(TM_RES // WINDOW):
        rows = slice(WINDOW * bq, WINDOW * (bq + 1))
        if bq == 0:
            kv_prev = zkv_ref[prev, TM_RES - WINDOW:TM_RES, :]
        else:
            kv_prev = zkv_ref[cur, WINDOW * (bq - 1):WINDOW * bq, :]
        kvcat = jnp.concatenate([kv_prev, zkv_ref[cur, rows, :]], axis=0)
        q = zq_ref[qslot, rows, :]
        for kv in range(N_KV):
            heads = range(GRP * kv, GRP * (kv + 1))
            kh = kvcat[:, HEAD_DIM * kv:HEAD_DIM * (kv + 1)]
            vh = kvcat[:, KV_WIDTH + HEAD_DIM * kv:KV_WIDTH + HEAD_DIM * (kv + 1)]
            qg = jnp.concatenate([q[:, HEAD_DIM * h:HEAD_DIM * (h + 1)] for h in heads], axis=0)
            s_all = lax.dot_general(qg, kh, (((1,), (1,)), ((), ())), preferred_element_type=F32)
            e_all, denoms = [], []
            for i, h in enumerate(heads):
                sink = sink_ref[l, h]
                s = s_all[WINDOW * i:WINDOW * (i + 1)] + bias_ref[h]
                if bq == 0:
                    s = s + hide_prev
                m = jnp.maximum(jnp.max(s, axis=-1, keepdims=True), sink)
                e = jnp.exp(s - m)
                denoms.append(jnp.sum(e, axis=-1, keepdims=True) + jnp.exp(sink - m))
                e_all.append(e.astype(BF16))
            o_all = jnp.dot(jnp.concatenate(e_all, axis=0), vh, preferred_element_type=F32)
            for i, h in enumerate(heads):
                o_ref[rows, HEAD_DIM * h:HEAD_DIM * (h + 1)] = o_all[WINDOW * i:WINDOW * (i + 1)] / denoms[i]


def _inproj_attn_kernel(sink_ref, x_ref, xs_ref, g_ref, w_ref, o_ref, os_ref, oa_ref,
                        wres_ref, h0_ref, zq_ref, zkv_ref, bias_ref, *, l, nw, nt, tiles_per_seq):
    s = pl.program_id(0)
    kv_cols = slice(Q_WIDTH, Q_WIDTH + 2 * KV_WIDTH)

    @pl.when(s == 0)
    def _():
        h0_ref[...] = _rms(x_ref[...], g_ref[...]).astype(BF16)
        zkv_ref[...] = jnp.zeros_like(zkv_ref)
        qi, kj = _iota2((WINDOW, 2 * WINDOW))
        dist_i = qi - kj + WINDOW
        band = (dist_i >= 0) & (dist_i <= WINDOW)
        dist = dist_i.astype(F32)
        for h in range(N_HEADS):
            bias_ref[h] = jnp.where(band, -(_SLOPES[h] * dist), NEG_INF)

    def tile0(cols, chunk):
        o_ref[:, cols] = jnp.dot(h0_ref[...], chunk, preferred_element_type=F32)

    _on_weight_chunks(s, w_ref, wres_ref, nw, tile0)

    def keep_for_attention(t):
        zq_ref[lax.rem(t, 2)] = (o_ref[:, :Q_WIDTH] * (HEAD_DIM ** -0.5)).astype(BF16)
        zkv_ref[lax.rem(t, 3)] = o_ref[:, kv_cols].astype(BF16)

    @pl.when(s == nw - 1)
    def _():
        keep_for_attention(jnp.int32(0))

    @pl.when(_is_tile_step(s, nw, nt))
    def _():
        t = s - nw + 1
        _attend_tile(t - 1, l, tiles_per_seq, sink_ref, bias_ref, zq_ref, zkv_ref, oa_ref)
        h = _rms(x_ref[...], g_ref[...]).astype(BF16)
        o_ref[...] = jnp.dot(h, wres_ref[...], preferred_element_type=F32)
        keep_for_attention(t)

    @pl.when(s == nw + nt - 1)
    def _():
        _attend_tile(nt - 1, l, tiles_per_seq, sink_ref, bias_ref, zq_ref, zkv_ref, oa_ref)
        h = _rms(xs_ref[...], g_ref[...]).astype(BF16)
        os_ref[...] = jnp.dot(h, wres_ref[...], preferred_element_type=F32)


def _inproj_attn(l, x, xs, g, w, sinks, t):
    m, ns = x.shape[0], xs.shape[0]
    n = w.shape[2]
    nw, nt = n // TN_W, m // TM_RES
    tile = lambda s: (_tile_index(s, nw, nt), 0)
    attn_tile = lambda s: (jnp.minimum(jnp.maximum(s - nw, 0), nt - 1), 0)
    const = lambda s: (0, 0)
    return pl.pallas_call(
        functools.partial(_inproj_attn_kernel, l=l, nw=nw, nt=nt, tiles_per_seq=t // TM_RES),
        grid=(nw + nt,),
        in_specs=[
            pl.BlockSpec(memory_space=pltpu.SMEM),
            pl.BlockSpec((TM_RES, D_MODEL), tile),
            pl.BlockSpec((ns, D_MODEL), const),
            pl.BlockSpec((None, 1, D_MODEL), lambda s: (l, 0, 0)),
            pl.BlockSpec((None, D_MODEL, TN_W), lambda s: (l, 0, jnp.minimum(s, nw - 1))),
        ],
        out_specs=[pl.BlockSpec((TM_RES, n), tile), pl.BlockSpec((ns, n), const),
                   pl.BlockSpec((TM_RES, Q_WIDTH), attn_tile)],
        out_shape=[jax.ShapeDtypeStruct((m, n), F32), jax.ShapeDtypeStruct((ns, n), F32),
                   jax.ShapeDtypeStruct((m, Q_WIDTH), F32)],
        scratch_shapes=[pltpu.VMEM((D_MODEL, n), BF16), pltpu.VMEM((TM_RES, D_MODEL), BF16),
                        pltpu.VMEM((2, TM_RES, Q_WIDTH), BF16), pltpu.VMEM((3, TM_RES, 2 * KV_WIDTH), BF16),
                        pltpu.VMEM((N_HEADS, WINDOW, 2 * WINDOW), F32)],
        compiler_params=_params(("arbitrary",), vmem_limit=VMEM_LIMIT_FUSED),
        name="inproj_attn",
    )(sinks, x, xs, g, w)


def _attn_sample_kernel(q_ref, kvn_ref, ck_ref, cv_ref, sink_ref, slope_ref, o_ref):
    dist = (WINDOW - lax.broadcasted_iota(jnp.int32, (GRP, WINDOW), 1)).astype(F32)
    for n in range(q_ref.shape[0]):
        ck = ck_ref[n].astype(BF16)
        cv = cv_ref[n].astype(BF16)
        kn = kvn_ref[n, :, :KV_WIDTH]
        vn = kvn_ref[n, :, KV_WIDTH:]
        for kv in range(N_KV):
            lo, hi = HEAD_DIM * kv, HEAD_DIM * (kv + 1)
            qk = q_ref[n, GRP * kv:GRP * (kv + 1), :]
            slope = slope_ref[GRP * kv:GRP * (kv + 1), :]
            sink = sink_ref[GRP * kv:GRP * (kv + 1), :]
            s_c = lax.dot_general(qk.astype(BF16), ck[:, lo:hi], (((1,), (1,)), ((), ())),
                                  preferred_element_type=F32) * (HEAD_DIM ** -0.5)
            s_c = s_c - slope * dist
            s_n = jnp.sum(qk * kn[:, lo:hi], axis=-1, keepdims=True) * (HEAD_DIM ** -0.5)
            m = jnp.maximum(jnp.maximum(jnp.max(s_c, axis=-1, keepdims=True), s_n), sink)
            e_c = jnp.exp(s_c - m)
            e_n = jnp.exp(s_n - m)
            denom = jnp.sum(e_c, axis=-1, keepdims=True) + e_n + jnp.exp(sink - m)
            o = jnp.dot(e_c.astype(BF16), cv[:, lo:hi], preferred_element_type=F32) + e_n * vn[:, lo:hi]
            o_ref[n, GRP * kv:GRP * (kv + 1), :] = o / denom


def _attn_sample(l, q3, kvn, ck, cv, sink_col, slope_col):
    n = q3.shape[0]
    sp = SAMPLE_PER_STEP
    return pl.pallas_call(
        _attn_sample_kernel,
        grid=(n // sp,),
        in_specs=[
            pl.BlockSpec((sp, N_HEADS, HEAD_DIM), lambda i: (i, 0, 0)),
            pl.BlockSpec((sp, 1, 2 * KV_WIDTH), lambda i: (i, 0, 0)),
            pl.BlockSpec((None, sp, WINDOW, KV_WIDTH), lambda i: (l, i, 0, 0)),
            pl.BlockSpec((None, sp, WINDOW, KV_WIDTH), lambda i: (l, i, 0, 0)),
            pl.BlockSpec((None, N_HEADS, 1), lambda i: (l, 0, 0)),
            pl.BlockSpec((N_HEADS, 1), lambda i: (0, 0)),
        ],
        out_specs=pl.BlockSpec((sp, N_HEADS, HEAD_DIM), lambda i: (i, 0, 0)),
        out_shape=jax.ShapeDtypeStruct((n, N_HEADS, HEAD_DIM), F32),
        compiler_params=_params(("arbitrary",)),
        name="attn_sample",
    )(q3, kvn, ck, cv, sink_col, slope_col)


def _discretise(a_re, a_im, log_dt):
    dt = jnp.exp(log_dt)
    dta_re = dt * a_re
    dta_im = dt * a_im
    mag = jnp.exp(dta_re)
    return mag * jnp.cos(dta_im), mag * jnp.sin(dta_im)


def _ssm_prep_kernel(are_ref, aim_ref, ldt_ref, bre_ref, bim_ref, cre_ref, cim_ref, wg_ref,
                     ared_ref, aimd_ref, ldtd_ref,
                     wb_ref, wbd_ref, wcr_ref, wci_ref, wglu_ref, ad_ref, ar_ref):
    copy_dot = lambda a, b: jnp.dot(a.astype(BF16), b, preferred_element_type=F32)

    a_re = are_ref[...]
    a_im = aim_ref[...]
    ab_re, ab_im = _discretise(a_re, a_im, ldt_ref[...])
    ar_ref[0] = ab_re
    ar_ref[1] = ab_im
    den = a_re * a_re + a_im * a_im
    f_re = ((ab_re - 1.0) * a_re + ab_im * a_im) / den
    f_im = (ab_im * a_re - (ab_re - 1.0) * a_im) / den
    r, _ = _iota2((LANES, SLAB_ST))
    f_t = jnp.where(r == 0, f_re, jnp.where(r == 1, f_im, 0.0)).T
    f_re = f_t[:, 0:1]
    f_im = f_t[:, 1:2]

    b_re = bre_ref[...]
    b_im = bim_ref[...]
    r, c = _iota2((SSM_CG, SLAB_CH))
    tile_c = (r == (c & (SSM_CG - 1))).astype(BF16)
    r, c = _iota2((SLAB_ST, SLAB_CH))
    diag = (r >> 6) == (c >> 4)
    bbt_re = jnp.where(diag, copy_dot(f_re * b_re - f_im * b_im, tile_c), 0.0)
    bbt_im = jnp.where(diag, copy_dot(f_re * b_im + f_im * b_re, tile_c), 0.0)
    wb_re = bbt_re.T
    wb_im = bbt_im.T
    wb_ref[:, :SLAB_ST] = wb_re.astype(BF16)
    wb_ref[:, SLAB_ST:] = wb_im.astype(BF16)
    r, _ = _iota2((SLAB_CH, LANES))
    own = [(r >> 5) == k for k in range(ST_TILES)]
    fold = lambda w: sum(jnp.where(own[k], w[:, LANES * k:LANES * (k + 1)], 0.0) for k in range(ST_TILES))
    wbd_ref[:, :LANES] = fold(wb_re).astype(BF16)
    wbd_ref[:, LANES:] = fold(wb_im).astype(BF16)

    r, c = _iota2((SSM_STATE, SLAB_ST))
    tile_p = (r == (c & (SSM_STATE - 1))).astype(BF16)
    r, c = _iota2((SLAB_CH, SLAB_ST))
    diag = (r >> 4) == (c >> 6)
    wcr_ref[...] = jnp.where(diag, copy_dot(cre_ref[...], tile_p), 0.0).T.astype(BF16)
    wci_ref[...] = jnp.where(diag, copy_dot(cim_ref[...], tile_p), 0.0).T.astype(BF16)

    r, c = _iota2((2 * SSM_CG, 2 * SLAB_CH))
    tile_e = (r == (c & (SSM_CG - 1)) + SSM_CG * (c >> 8)).astype(BF16)
    r, c = _iota2((SLAB_CH, 2 * SLAB_CH))
    diag = (r >> 4) == ((c & (SLAB_CH - 1)) >> 4)
    wglu_ref[...] = jnp.where(diag, copy_dot(wg_ref[...], tile_e), 0.0).astype(BF16)

    abd_re, abd_im = _discretise(ared_ref[...], aimd_ref[...], ldtd_ref[...])
    ad_ref[0] = abd_re
    ad_ref[1] = abd_im


def _ssm_prep(a_re, a_im, log_dt, b_re, b_im, c_re, c_im, w_glu):
    gp = SSM_GROUPS * SSM_STATE
    gc = SSM_GROUPS * SSM_CG
    ldt_gp = jnp.repeat(log_dt, SSM_STATE, axis=1)
    row = lambda a: a.reshape(DEPTH, N_SLABS, 1, SLAB_ST)
    dense = lambda a: a.reshape(DEPTH, N_SLABS, SUBLANES, LANES)
    blk = lambda rows, cols: pl.BlockSpec((None, rows, cols), lambda l, s: (l, s, 0))
    rblk = pl.BlockSpec((None, None, 1, SLAB_ST), lambda l, s: (l, s, 0, 0))
    dblk = pl.BlockSpec((None, None, SUBLANES, LANES), lambda l, s: (l, s, 0, 0))
    oblk = lambda rows, cols: pl.BlockSpec((None, None, rows, cols), lambda l, s: (l, s, 0, 0))
    return pl.pallas_call(
        _ssm_prep_kernel,
        grid=(DEPTH, N_SLABS),
        in_specs=[rblk, rblk, rblk,
                  blk(SLAB_ST, SSM_CG), blk(SLAB_ST, SSM_CG),
                  blk(SLAB_CH, SSM_STATE), blk(SLAB_CH, SSM_STATE), blk(SLAB_CH, 2 * SSM_CG),
                  dblk, dblk, dblk],
        out_specs=[oblk(SLAB_CH, 2 * SLAB_ST), oblk(SLAB_CH, 2 * LANES),
                   oblk(SLAB_ST, SLAB_CH), oblk(SLAB_ST, SLAB_CH), oblk(SLAB_CH, 2 * SLAB_CH),
                   pl.BlockSpec((None, None, 2, SUBLANES, LANES), lambda l, s: (l, s, 0, 0, 0)),
                   pl.BlockSpec((None, None, 2, 1, SLAB_ST), lambda l, s: (l, s, 0, 0, 0))],
        out_shape=[jax.ShapeDtypeStruct((DEPTH, N_SLABS, SLAB_CH, 2 * SLAB_ST), BF16),
                   jax.ShapeDtypeStruct((DEPTH, N_SLABS, SLAB_CH, 2 * LANES), BF16),
                   jax.ShapeDtypeStruct((DEPTH, N_SLABS, SLAB_ST, SLAB_CH), BF16),
                   jax.ShapeDtypeStruct((DEPTH, N_SLABS, SLAB_ST, SLAB_CH), BF16),
                   jax.ShapeDtypeStruct((DEPTH, N_SLABS, SLAB_CH, 2 * SLAB_CH), BF16),
                   jax.ShapeDtypeStruct((DEPTH, N_SLABS, 2, SUBLANES, LANES), F32),
                   jax.ShapeDtypeStruct((DEPTH, N_SLABS, 2, 1, SLAB_ST), F32)],
        compiler_params=_params(("arbitrary", "arbitrary")),
        name="ssm_prep",
    )(row(a_re), row(a_im), row(ldt_gp),
      b_re.reshape(DEPTH, gp, SSM_CG), b_im.reshape(DEPTH, gp, SSM_CG),
      c_re.reshape(DEPTH, gc, SSM_STATE), c_im.reshape(DEPTH, gc, SSM_STATE),
      w_glu.reshape(DEPTH, gc, 2 * SSM_CG),
      dense(a_re), dense(a_im), dense(ldt_gp))


def _ssm_tail(s_re, s_im, u, wcr, wci, d, wglu):
    y = (jnp.dot(s_re.astype(BF16), wcr, preferred_element_type=F32)
         - jnp.dot(s_im.astype(BF16), wci, preferred_element_type=F32)
         + d * u)
    z = jnp.dot(_gelu_tanh(y).astype(BF16), wglu, preferred_element_type=F32)
    return z[:, :SLAB_CH] * jax.nn.sigmoid(z[:, SLAB_CH:])


def _ssm_prompt_kernel(*refs):
    u_refs = refs[:SLABS_PER_STEP]
    (wbd_ref, a_ref, wcr_ref, wci_ref, d_ref, wglu_ref,
     o_ref, sre_ref, sim_ref, st_ref, carry_ref) = refs[SLABS_PER_STEP:]
    nb = u_refs[0].shape[0]
    chains = [(p, b) for p in range(SLABS_PER_STEP) for b in range(nb)]

    @pl.when(pl.program_id(1) == 0)
    def _():
        carry_ref[...] = jnp.zeros_like(carry_ref)

    sub, ch = _iota2((SUBLANES, SLAB_CH))
    own = sub == (ch >> 5)
    for p, b in chains:
        lhs = jnp.where(own[None], u_refs[p][b][:, None, :], 0.0).astype(BF16)
        bu = jnp.dot(lhs.reshape(SUBLANES * SSM_CHUNK, SLAB_CH), wbd_ref[p],
                     preferred_element_type=F32)
        st_ref[p, b, 0] = bu[:, :LANES]
        st_ref[p, b, 1] = bu[:, LANES:]

    abar = [(a_ref[p, 0], a_ref[p, 1]) for p in range(SLABS_PER_STEP)]

    def step(t, carry):
        row = pl.multiple_of(t * SUBLANES, SUBLANES)
        new = []
        for i, (p, b) in enumerate(chains):
            a_re, a_im = abar[p]
            sr, si = carry[2 * i], carry[2 * i + 1]
            nr = a_re * sr - a_im * si + st_ref[p, b, 0, pl.ds(row, SUBLANES), :]
            ni = a_re * si + a_im * sr + st_ref[p, b, 1, pl.ds(row, SUBLANES), :]
            st_ref[p, b, 0, pl.ds(row, SUBLANES), :] = nr
            st_ref[p, b, 1, pl.ds(row, SUBLANES), :] = ni
            new += [nr, ni]
        return tuple(new)

    init = tuple(carry_ref[p, b, j] for p, b in chains for j in range(2))
    fin = lax.fori_loop(0, SSM_CHUNK, step, init, unroll=2)
    for i, (p, b) in enumerate(chains):
        carry_ref[p, b, 0] = fin[2 * i]
        carry_ref[p, b, 1] = fin[2 * i + 1]
        sre_ref[p, b] = fin[2 * i]
        sim_ref[p, b] = fin[2 * i + 1]

    tiles = lambda p, b, j: jnp.concatenate(
        [st_ref[p, b, j, pl.ds(k, SSM_CHUNK, stride=SUBLANES), :] for k in range(ST_TILES)], axis=1)
    for p, b in chains:
        o_ref[b, :, SLAB_CH * p:SLAB_CH * (p + 1)] = _ssm_tail(
            tiles(p, b, 0), tiles(p, b, 1), u_refs[p][b], wcr_ref[p], wci_ref[p], d_ref[p], wglu_ref[p])


def _slab_specs(l):
    w4 = lambda rows, cols: pl.BlockSpec((None, None, rows, cols), lambda s, *_: (l, s, 0, 0))
    return dict(wb=w4(SLAB_CH, 2 * SLAB_ST), wc=w4(SLAB_ST, SLAB_CH),
                d=w4(1, SLAB_CH), wglu=w4(SLAB_CH, 2 * SLAB_CH))


def _ssm_prompt(l, z3, wbd, a_dense, wcr, wci, d_row, wglu):
    nb, t, _ = z3.shape
    sps = SLABS_PER_STEP
    ucol = (Q_WIDTH + 2 * KV_WIDTH) // SLAB_CH
    u_spec = lambda p: pl.BlockSpec((nb, SSM_CHUNK, SLAB_CH), lambda s, c: (0, c, ucol + sps * s + p))
    w4 = lambda rows, cols: pl.BlockSpec((None, sps, rows, cols), lambda s, c: (l, s, 0, 0))
    st_blk = pl.BlockSpec((sps, nb, SUBLANES, LANES), lambda s, c: (s, 0, 0, 0))
    return pl.pallas_call(
        _ssm_prompt_kernel,
        grid=(N_SLABS // sps, t // SSM_CHUNK),
        in_specs=[u_spec(p) for p in range(sps)] + [
            w4(SLAB_CH, 2 * LANES),
            pl.BlockSpec((None, sps, 2, SUBLANES, LANES), lambda s, c: (l, s, 0, 0, 0)),
            w4(SLAB_ST, SLAB_CH), w4(SLAB_ST, SLAB_CH), w4(1, SLAB_CH), w4(SLAB_CH, 2 * SLAB_CH),
        ],
        out_specs=[pl.BlockSpec((nb, SSM_CHUNK, sps * SLAB_CH), lambda s, c: (0, c, s)), st_blk, st_blk],
        out_shape=[
            jax.ShapeDtypeStruct((nb, t, SSM_WIDTH), F32),
            jax.ShapeDtypeStruct((N_SLABS, nb, SUBLANES, LANES), F32),
            jax.ShapeDtypeStruct((N_SLABS, nb, SUBLANES, LANES), F32),
        ],
        scratch_shapes=[
            pltpu.VMEM((sps, nb, 2, SUBLANES * SSM_CHUNK, LANES), F32),
            pltpu.VMEM((sps, nb, 2, SUBLANES, LANES), F32),
        ],
        compiler_params=_params(("arbitrary", "arbitrary")),
        name="ssm_prompt",
    )(*([z3] * sps), wbd, a_dense, wcr, wci, d_row, wglu)


def _ssm_sample_kernel(u_ref, wb_ref, a_ref, s0r_ref, s0i_ref, wcr_ref, wci_ref, d_ref, wglu_ref,
                       o_ref, sre_ref, sim_ref):
    u = u_ref[...]
    bu = jnp.dot(u.astype(BF16), wb_ref[...], preferred_element_type=F32)
    a_re = a_ref[0]
    a_im = a_ref[1]
    s0r = s0r_ref[...]
    s0i = s0i_ref[...]
    nr = bu[:, :SLAB_ST] + a_re * s0r - a_im * s0i
    ni = bu[:, SLAB_ST:] + a_re * s0i + a_im * s0r
    sre_ref[...] = nr
    sim_ref[...] = ni
    o_ref[...] = _ssm_tail(nr, ni, u, wcr_ref[...], wci_ref[...], d_ref[...], wglu_ref[...])


def _ssm_sample(l, z, wb, a_row, s0r, s0i, wcr, wci, d_row, wglu):
    n = z.shape[0]
    ucol = (Q_WIDTH + 2 * KV_WIDTH) // SLAB_CH
    sp = _slab_specs(l)
    st_in = pl.BlockSpec((None, n, SLAB_ST), lambda s: (l, 0, s))
    st_out = pl.BlockSpec((n, SLAB_ST), lambda s: (0, s))
    return pl.pallas_call(
        _ssm_sample_kernel,
        grid=(N_SLABS,),
        in_specs=[
            pl.BlockSpec((n, SLAB_CH), lambda s: (0, ucol + s)),
            sp["wb"],
            pl.BlockSpec((None, None, 2, 1, SLAB_ST), lambda s: (l, s, 0, 0, 0)),
            st_in, st_in, sp["wc"], sp["wc"], sp["d"], sp["wglu"],
        ],
        out_specs=[pl.BlockSpec((n, SLAB_CH), lambda s: (0, s)), st_out, st_out],
        out_shape=[
            jax.ShapeDtypeStruct((n, SSM_WIDTH), F32),
            jax.ShapeDtypeStruct((n, SSM_GROUPS * SSM_STATE), F32),
            jax.ShapeDtypeStruct((n, SSM_GROUPS * SSM_STATE), F32),
        ],
        compiler_params=_params(("arbitrary",)),
        name="ssm_sample",
    )(z, wb, a_row, s0r, s0i, wcr, wci, d_row, wglu)


def _outproj_kernel(a_ref, m_ref, x_ref, as_ref, ms_ref, xs_ref, ga_ref, gs_ref, gp_ref, w_ref,
                    o_ref, os_ref, wres_ref, m0_ref, y0_ref, *, nw, nt):
    s = pl.program_id(0)

    def merge(attn, ssm):
        return jnp.concatenate([_rms(attn, ga_ref[...]).astype(BF16),
                                _rms(ssm, gs_ref[...]).astype(BF16)], axis=1)

    def tile(attn, ssm, x):
        y = jnp.dot(merge(attn, ssm), wres_ref[...], preferred_element_type=F32)
        return x + _rms(y, gp_ref[...])

    @pl.when(s == 0)
    def _():
        m0_ref[...] = merge(a_ref[...], m_ref[...])

    def tile0(cols, chunk):
        y0_ref[:, cols] = jnp.dot(m0_ref[...], chunk, preferred_element_type=F32)

    _on_weight_chunks(s, w_ref, wres_ref, nw, tile0)

    @pl.when(s == nw - 1)
    def _():
        o_ref[...] = x_ref[...] + _rms(y0_ref[...], gp_ref[...])

    @pl.when(_is_tile_step(s, nw, nt))
    def _():
        o_ref[...] = tile(a_ref[...], m_ref[...], x_ref[...])

    @pl.when(s == nw + nt - 1)
    def _():
        os_ref[...] = tile(as_ref[...], ms_ref[...], xs_ref[...])


def _outproj(l, attn, ssm, x, attn_s, ssm_s, xs, ga, gs, gp, w):
    m, ns = x.shape[0], xs.shape[0]
    nw, nt = D_MODEL // TN_W, m // TM_RES
    tile = lambda s: (_tile_index(s, nw, nt), 0)
    const = lambda s: (0, 0)
    gain = lambda n: pl.BlockSpec((None, 1, n), lambda s: (l, 0, 0))
    return pl.pallas_call(
        functools.partial(_outproj_kernel, nw=nw, nt=nt),
        grid=(nw + nt,),
        in_specs=[
            pl.BlockSpec((TM_RES, Q_WIDTH), tile),
            pl.BlockSpec((TM_RES, SSM_WIDTH), tile),
            pl.BlockSpec((TM_RES, D_MODEL), tile),
            pl.BlockSpec((ns, Q_WIDTH), const),
            pl.BlockSpec((ns, SSM_WIDTH), const),
            pl.BlockSpec((ns, D_MODEL), const),
            gain(Q_WIDTH), gain(SSM_WIDTH), gain(D_MODEL),
            pl.BlockSpec((None, D_MODEL, TN_W), lambda s: (l, 0, jnp.minimum(s, nw - 1))),
        ],
        out_specs=[pl.BlockSpec((TM_RES, D_MODEL), tile), pl.BlockSpec((ns, D_MODEL), const)],
        out_shape=[jax.ShapeDtypeStruct((m, D_MODEL), F32), jax.ShapeDtypeStruct((ns, D_MODEL), F32)],
        scratch_shapes=[pltpu.VMEM((D_MODEL, D_MODEL), BF16), pltpu.VMEM((TM_RES, D_MODEL), BF16),
                        pltpu.VMEM((TM_RES, D_MODEL), F32)],
        compiler_params=_params(("arbitrary",)),
        name="outproj",
    )(attn, ssm, x, attn_s, ssm_s, xs, ga, gs, gp, w)


def _ffn_kernel(x_ref, xs_ref, g1_ref, wg_ref, wu_ref, wd_ref, g2_ref, o_ref, os_ref, h_ref):
    i = pl.program_id(0)
    j = pl.program_id(1)
    nj = pl.num_programs(1)
    tm = x_ref.shape[0]

    @pl.when(j == 0)
    def _():
        h_ref[:tm, :] = _rms(x_ref[...], g1_ref[...]).astype(BF16)
        o_ref[...] = jnp.zeros_like(o_ref)

    @pl.when((j == 0) & (i == 0))
    def _():
        h_ref[tm:, :] = _rms(xs_ref[...], g1_ref[...]).astype(BF16)
        os_ref[...] = jnp.zeros_like(os_ref)

    def swiglu_down(h):
        gate = jnp.dot(h, wg_ref[...].astype(BF16), preferred_element_type=F32)
        up = jnp.dot(h, wu_ref[...].astype(BF16), preferred_element_type=F32)
        act = (gate * jax.nn.sigmoid(gate) * up).astype(BF16)
        return jnp.dot(act, wd_ref[...].astype(BF16), preferred_element_type=F32)

    @pl.when(i == 0)
    def _():
        down = swiglu_down(h_ref[...])
        o_ref[...] += down[:tm]
        os_ref[...] += down[tm:]

    @pl.when(i > 0)
    def _():
        o_ref[...] += swiglu_down(h_ref[:tm, :])

    @pl.when(j == nj - 1)
    def _():
        o_ref[...] = x_ref[...] + _rms(o_ref[...], g2_ref[...])

    @pl.when((j == nj - 1) & (i == 0))
    def _():
        os_ref[...] = xs_ref[...] + _rms(os_ref[...], g2_ref[...])


def _ffn(l, x, xs, g1, wgu, wd, g2):
    m, ns = x.shape[0], xs.shape[0]
    nf = D_FF // TF_FFN
    gain = pl.BlockSpec((None, 1, D_MODEL), lambda i, j: (l, 0, 0))
    return pl.pallas_call(
        _ffn_kernel,
        grid=(m // TM_FFN, nf),
        in_specs=[
            pl.BlockSpec((TM_FFN, D_MODEL), lambda i, j: (i, 0), pipeline_mode=pl.Buffered(1)),
            pl.BlockSpec((ns, D_MODEL), lambda i, j: (0, 0)),
            gain,
            pl.BlockSpec((None, D_MODEL, TF_FFN), lambda i, j: (l, 0, j)),
            pl.BlockSpec((None, D_MODEL, TF_FFN), lambda i, j: (l, 0, nf + j)),
            pl.BlockSpec((None, TF_FFN, D_MODEL), lambda i, j: (l, j, 0)),
            gain,
        ],
        out_specs=[pl.BlockSpec((TM_FFN, D_MODEL), lambda i, j: (i, 0)),
                   pl.BlockSpec((ns, D_MODEL), lambda i, j: (0, 0))],
        out_shape=[jax.ShapeDtypeStruct((m, D_MODEL), F32), jax.ShapeDtypeStruct((ns, D_MODEL), F32)],
        scratch_shapes=[pltpu.VMEM((TM_FFN + ns, D_MODEL), BF16)],
        compiler_params=_params(("arbitrary", "arbitrary")),
        name="ffn",
    )(x, xs, g1, wgu, wgu, wd, g2)


def _ple_kernel(x_ref, pe_ref, xs_ref, pes_ref, wg_ref, wp_ref, o_ref, os_ref,
                wgres_ref, wpres_ref, x0_ref, pe0_ref, *, nw, nt):
    s = pl.program_id(0)

    def tile(x, pe):
        gate = jnp.dot(x.astype(BF16), wgres_ref[...], preferred_element_type=F32)
        proj = jnp.dot(pe.astype(BF16), wpres_ref[...], preferred_element_type=F32)
        return x + jax.nn.sigmoid(gate) * proj

    @pl.when(s == 0)
    def _():
        x0_ref[...] = x_ref[...].astype(BF16)
        pe0_ref[...] = pe_ref[...].astype(BF16)

    def tile0(cols, gate_chunk):
        proj_chunk = wp_ref[...].astype(BF16)
        wpres_ref[:, cols] = proj_chunk
        gate = jnp.dot(x0_ref[...], gate_chunk, preferred_element_type=F32)
        proj = jnp.dot(pe0_ref[...], proj_chunk, preferred_element_type=F32)
        o_ref[:, cols] = x_ref[:, cols] + jax.nn.sigmoid(gate) * proj

    _on_weight_chunks(s, wg_ref, wgres_ref, nw, tile0)

    @pl.when(_is_tile_step(s, nw, nt))
    def _():
        o_ref[...] = tile(x_ref[...], pe_ref[...])

    @pl.when(s == nw + nt - 1)
    def _():
        os_ref[...] = tile(xs_ref[...], pes_ref[...])


def _ple(l, x, pe, xs, pes, wg, wp):
    m, ns = x.shape[0], xs.shape[0]
    nw, nt = D_MODEL // TN_W, m // TM_RES
    tile = lambda s: (_tile_index(s, nw, nt), 0)
    const = lambda s: (0, 0)
    chunk = lambda s: (l, 0, jnp.minimum(s, nw - 1))
    return pl.pallas_call(
        functools.partial(_ple_kernel, nw=nw, nt=nt),
        grid=(nw + nt,),
        in_specs=[
            pl.BlockSpec((TM_RES, D_MODEL), tile),
            pl.BlockSpec((None, TM_RES, PLE_DIM), lambda s: (l, _tile_index(s, nw, nt), 0)),
            pl.BlockSpec((ns, D_MODEL), const),
            pl.BlockSpec((None, ns, PLE_DIM), lambda s: (l, 0, 0)),
            pl.BlockSpec((None, D_MODEL, TN_W), chunk),
            pl.BlockSpec((None, PLE_DIM, TN_W), chunk),
        ],
        out_specs=[pl.BlockSpec((TM_RES, D_MODEL), tile), pl.BlockSpec((ns, D_MODEL), const)],
        out_shape=[jax.ShapeDtypeStruct((m, D_MODEL), F32), jax.ShapeDtypeStruct((ns, D_MODEL), F32)],
        scratch_shapes=[pltpu.VMEM((D_MODEL, D_MODEL), BF16), pltpu.VMEM((PLE_DIM, D_MODEL), BF16),
                        pltpu.VMEM((TM_RES, D_MODEL), BF16), pltpu.VMEM((TM_RES, PLE_DIM), BF16)],
        compiler_params=_params(("arbitrary",)),
        name="ple",
    )(x, pe, xs, pes, wg, wp)


def kernel(x_prompt, x_sample, cache_k, cache_v, state_ssm_re, state_ssm_im, p_prompt, p_sample,
           g_pre_mix, w_in, attn_sinks, ssm_a_re, ssm_a_im, ssm_log_dt, ssm_b_re, ssm_b_im,
           ssm_c_re, ssm_c_im, ssm_d, ssm_w_glu, g_attn_out, g_ssm_out, w_out, g_post_mix,
           g_pre_ffn, w_gate_up, w_down, g_post_ffn, w_ple_gate, w_ple_proj):
    nb, t, _ = x_prompt.shape
    ns = x_sample.shape[0]

    wb, wbd, wcr, wci, wglu, a_dense, a_row = _ssm_prep(ssm_a_re, ssm_a_im, ssm_log_dt, ssm_b_re, ssm_b_im,
                                                        ssm_c_re, ssm_c_im, ssm_w_glu)
    d_row = ssm_d.reshape(DEPTH, N_SLABS, 1, SLAB_CH)
    slope_col = jnp.asarray(_SLOPES, F32).reshape(N_HEADS, 1)
    sink_col = attn_sinks.reshape(DEPTH, N_HEADS, 1)
    gain = lambda g: g.reshape(DEPTH, 1, g.shape[-1])
    g_pre_mix, g_attn_out, g_ssm_out, g_post_mix, g_pre_ffn, g_post_ffn = map(
        gain, (g_pre_mix, g_attn_out, g_ssm_out, g_post_mix, g_pre_ffn, g_post_ffn))

    xp = x_prompt.reshape(nb * t, D_MODEL)
    xs = x_sample.reshape(ns, D_MODEL)
    pp = p_prompt.reshape(DEPTH, nb * t, PLE_DIM)
    ps = p_sample.reshape(DEPTH, ns, PLE_DIM)
    s0r = state_ssm_re.reshape(DEPTH, ns, SSM_GROUPS * SSM_STATE)
    s0i = state_ssm_im.reshape(DEPTH, ns, SSM_GROUPS * SSM_STATE)
    ck = cache_k.reshape(DEPTH, ns, WINDOW, KV_WIDTH)
    cv = cache_v.reshape(DEPTH, ns, WINDOW, KV_WIDTH)

    kv_lo, kv_hi = Q_WIDTH, Q_WIDTH + 2 * KV_WIDTH

    kvp_l, srp_l, sip_l, kvs_l, srs_l, sis_l = [], [], [], [], [], []
    for l in range(DEPTH):
        z, zs, attn = _inproj_attn(l, xp, xs, g_pre_mix, w_in, attn_sinks, t)
        z3 = z.reshape(nb, t, IN_WIDTH)

        ssm, sre, sim = _ssm_prompt(l, z3, wbd, a_dense, wcr, wci, d_row, wglu)

        kvn = zs[:, kv_lo:kv_hi].reshape(ns, 1, 2 * KV_WIDTH)
        attn_s = _attn_sample(l, zs[:, :Q_WIDTH].reshape(ns, N_HEADS, HEAD_DIM), kvn,
                              ck, cv, sink_col, slope_col)
        ssm_s, srs, sis = _ssm_sample(l, zs, wb, a_row, s0r, s0i, wcr, wci, d_row, wglu)

        x1, x1s = _outproj(l, attn, ssm.reshape(nb * t, SSM_WIDTH), xp,
                           attn_s.reshape(ns, Q_WIDTH), ssm_s, xs,
                           g_attn_out, g_ssm_out, g_post_mix, w_out)
        x2, x2s = _ffn(l, x1, x1s, g_pre_ffn, w_gate_up, w_down, g_post_ffn)
        xp, xs = _ple(l, x2, pp, x2s, ps, w_ple_gate, w_ple_proj)

        kvp_l.append(z3[:, t - WINDOW:, kv_lo:kv_hi])
        kvs_l.append(kvn)
        srp_l.append(sre)
        sip_l.append(sim)
        srs_l.append(srs)
        sis_l.append(sis)

    heads = lambda a: a.reshape(a.shape[:-1] + (N_KV, HEAD_DIM))
    kvp = jnp.stack(kvp_l)
    kvs = jnp.stack(kvs_l)
    new_ks = jnp.concatenate([ck[:, :, 1:], kvs[..., :KV_WIDTH]], axis=2)
    new_vs = jnp.concatenate([cv[:, :, 1:], kvs[..., KV_WIDTH:]], axis=2)
    unslab = lambda s: jnp.swapaxes(jnp.stack(s), 1, 2).reshape(DEPTH, nb, SSM_GROUPS, SSM_STATE)
    states = lambda s: jnp.stack(s).reshape(DEPTH, ns, SSM_GROUPS, SSM_STATE)
    return (xp.reshape(nb, t, D_MODEL), xs.reshape(ns, 1, D_MODEL),
            heads(kvp[..., :KV_WIDTH]), heads(kvp[..., KV_WIDTH:]), unslab(srp_l), unslab(sip_l),
            heads(new_ks), heads(new_vs), states(srs_l), states(sis_l))
```

```python
import functools
import math

import numpy as np
import jax
import jax.numpy as jnp
from jax import lax
from jax.experimental import pallas as pl
from jax.experimental.pallas import tpu as pltpu

F32 = jnp.float32
BF16 = jnp.bfloat16

D_MODEL = 2048
DEPTH = 4
HEAD_DIM = 64
N_HEADS = 16
N_KV = 2
GRP = N_HEADS // N_KV
WINDOW = 128
Q_WIDTH = N_HEADS * HEAD_DIM
KV_WIDTH = N_KV * HEAD_DIM
SSM_WIDTH = 1024
SSM_CG = 16
SSM_GROUPS = 64
SSM_STATE = 64
IN_WIDTH = Q_WIDTH + 2 * KV_WIDTH + SSM_WIDTH
D_FF = 5632
PLE_DIM = 256
EPS = 1e-6
NEG_INF = -1e30

LANES = 128
SUBLANES = 8

SLAB_GROUPS = 16
N_SLABS = SSM_GROUPS // SLAB_GROUPS
SLAB_CH = SLAB_GROUPS * SSM_CG
SLAB_ST = SLAB_GROUPS * SSM_STATE
ST_TILES = SLAB_ST // LANES
SSM_CHUNK = 256
SLABS_PER_STEP = 2

TM_RES = 512
TN_W = 256
TM_FFN = 1024
TF_FFN = 256
SAMPLE_PER_STEP = 8

VMEM_LIMIT = 56 * 1024 * 1024
VMEM_LIMIT_FUSED = 60 * 1024 * 1024

_SLOPES = [float(v) for v in
           (2.0 ** (-8.0 * np.arange(1, N_HEADS + 1, dtype=np.float32) / N_HEADS)).astype(np.float32)]


def _rms(x, g):
    var = jnp.mean(x * x, axis=-1, keepdims=True)
    return x * lax.rsqrt(var + EPS) * g


def _iota2(shape):
    return (lax.broadcasted_iota(jnp.int32, shape, 0), lax.broadcasted_iota(jnp.int32, shape, 1))


def _gelu_tanh(x):
    c = math.sqrt(2.0 / math.pi)
    return 0.5 * x * (1.0 + jnp.tanh(c * (x + 0.044715 * (x * x * x))))


def _params(sem, vmem_limit=VMEM_LIMIT):
    return pltpu.CompilerParams(dimension_semantics=sem, vmem_limit_bytes=vmem_limit)


def _tile_index(s, nw, nt):
    return jnp.minimum(jnp.maximum(s - nw + 1, 0), nt - 1)


def _is_tile_step(s, nw, nt):
    return (s >= nw) & (s < nw + nt - 1)


def _on_weight_chunks(s, w_ref, wres_ref, nw, apply):
    tn = w_ref.shape[1]
    for c in range(nw):
        @pl.when(s == c)
        def _(c=c):
            chunk = w_ref[...].astype(BF16)
            wres_ref[:, c * tn:(c + 1) * tn] = chunk
            apply(slice(c * tn, (c + 1) * tn), chunk)


def _attend_tile(tau, l, tiles_per_seq, sink_ref, bias_ref, zq_ref, zkv_ref, o_ref):
    tau = jnp.asarray(tau, jnp.int32)
    qslot = lax.rem(tau, 2)
    cur = lax.rem(tau, 3)
    prev = lax.rem(tau + 2, 3)
    seq_start = lax.rem(tau, tiles_per_seq) == 0
    lane = lax.broadcasted_iota(jnp.int32, (1, 2 * WINDOW), 1)
    hide_prev = jnp.where(seq_start & (lane < WINDOW), NEG_INF, 0.0)
    for bq in range(TM_RES // WINDOW):
        rows = slice(WINDOW * bq, WINDOW * (bq + 1))
        if bq == 0:
            kv_prev = zkv_ref[prev, TM_RES - WINDOW:TM_RES, :]
        else:
            kv_prev = zkv_ref[cur, WINDOW * (bq - 1):WINDOW * bq, :]
        kvcat = jnp.concatenate([kv_prev, zkv_ref[cur, rows, :]], axis=0)
        q = zq_ref[qslot, rows, :]
        for kv in range(N_KV):
            heads = range(GRP * kv, GRP * (kv + 1))
            kh = kvcat[:, HEAD_DIM * kv:HEAD_DIM * (kv + 1)]
            vh = kvcat[:, KV_WIDTH + HEAD_DIM * kv:KV_WIDTH + HEAD_DIM * (kv + 1)]
            qg = jnp.concatenate([q[:, HEAD_DIM * h:HEAD_DIM * (h + 1)] for h in heads], axis=0)
            s_all = lax.dot_general(qg, kh, (((1,), (1,)), ((), ())), preferred_element_type=F32)
            e_all, denoms = [], []
            for i, h in enumerate(heads):
                sink = sink_ref[l, h]
                s = s_all[WINDOW * i:WINDOW * (i + 1)] + bias_ref[h]
                if bq == 0:
                    s = s + hide_prev
                m = jnp.maximum(jnp.max(s, axis=-1, keepdims=True), sink)
                e = jnp.exp(s - m)
                denoms.append(jnp.sum(e, axis=-1, keepdims=True) + jnp.exp(sink - m))
                e_all.append(e.astype(BF16))
            o_all = jnp.dot(jnp.concatenate(e_all, axis=0), vh, preferred_element_type=F32)
            for i, h in enumerate(heads):
                o_ref[rows, HEAD_DIM * h:HEAD_DIM * (h + 1)] = o_all[WINDOW * i:WINDOW * (i + 1)] / denoms[i]


def _inproj_attn_kernel(sink_ref, x_ref, xs_ref, g_ref, w_ref, o_ref, os_ref, oa_ref,
                        wres_ref, h0_ref, zq_ref, zkv_ref, bias_ref, *, l, nw, nt, tiles_per_seq):
    s = pl.program_id(0)
    kv_cols = slice(Q_WIDTH, Q_WIDTH + 2 * KV_WIDTH)

    @pl.when(s == 0)
    def _():
        h0_ref[...] = _rms(x_ref[...], g_ref[...]).astype(BF16)
        zkv_ref[...] = jnp.zeros_like(zkv_ref)
        qi, kj = _iota2((WINDOW, 2 * WINDOW))
        dist_i = qi - kj + WINDOW
        band = (dist_i >= 0) & (dist_i <= WINDOW)
        dist = dist_i.astype(F32)
        for h in range(N_HEADS):
            bias_ref[h] = jnp.where(band, -(_SLOPES[h] * dist), NEG_INF)

    def tile0(cols, chunk):
        o_ref[:, cols] = jnp.dot(h0_ref[...], chunk, preferred_element_type=F32)

    _on_weight_chunks(s, w_ref, wres_ref, nw, tile0)

    def keep_for_attention(t):
        zq_ref[lax.rem(t, 2)] = (o_ref[:, :Q_WIDTH] * (HEAD_DIM ** -0.5)).astype(BF16)
        zkv_ref[lax.rem(t, 3)] = o_ref[:, kv_cols].astype(BF16)

    @pl.when(s == nw - 1)
    def _():
        keep_for_attention(jnp.int32(0))

    @pl.when(_is_tile_step(s, nw, nt))
    def _():
        t = s - nw + 1
        _attend_tile(t - 1, l, tiles_per_seq, sink_ref, bias_ref, zq_ref, zkv_ref, oa_ref)
        h = _rms(x_ref[...], g_ref[...]).astype(BF16)
        o_ref[...] = jnp.dot(h, wres_ref[...], preferred_element_type=F32)
        keep_for_attention(t)

    @pl.when(s == nw + nt - 1)
    def _():
        _attend_tile(nt - 1, l, tiles_per_seq, sink_ref, bias_ref, zq_ref, zkv_ref, oa_ref)
        h = _rms(xs_ref[...], g_ref[...]).astype(BF16)
        os_ref[...] = jnp.dot(h, wres_ref[...], preferred_element_type=F32)


def _inproj_attn(l, x, xs, g, w, sinks, t):
    m, ns = x.shape[0], xs.shape[0]
    n = w.shape[2]
    nw, nt = n // TN_W, m // TM_RES
    tile = lambda s: (_tile_index(s, nw, nt), 0)
    attn_tile = lambda s: (jnp.minimum(jnp.maximum(s - nw, 0), nt - 1), 0)
    const = lambda s: (0, 0)
    return pl.pallas_call(
        functools.partial(_inproj_attn_kernel, l=l, nw=nw, nt=nt, tiles_per_seq=t // TM_RES),
        grid=(nw + nt,),
        in_specs=[
            pl.BlockSpec(memory_space=pltpu.SMEM),
            pl.BlockSpec((TM_RES, D_MODEL), tile),
            pl.BlockSpec((ns, D_MODEL), const),
            pl.BlockSpec((None, 1, D_MODEL), lambda s: (l, 0, 0)),
            pl.BlockSpec((None, D_MODEL, TN_W), lambda s: (l, 0, jnp.minimum(s, nw - 1))),
        ],
        out_specs=[pl.BlockSpec((TM_RES, n), tile), pl.BlockSpec((ns, n), const),
                   pl.BlockSpec((TM_RES, Q_WIDTH), attn_tile)],
        out_shape=[jax.ShapeDtypeStruct((m, n), F32), jax.ShapeDtypeStruct((ns, n), F32),
                   jax.ShapeDtypeStruct((m, Q_WIDTH), F32)],
        scratch_shapes=[pltpu.VMEM((D_MODEL, n), BF16), pltpu.VMEM((TM_RES, D_MODEL), BF16),
                        pltpu.VMEM((2, TM_RES, Q_WIDTH), BF16), pltpu.VMEM((3, TM_RES, 2 * KV_WIDTH), BF16),
                        pltpu.VMEM((N_HEADS, WINDOW, 2 * WINDOW), F32)],
        compiler_params=_params(("arbitrary",), vmem_limit=VMEM_LIMIT_FUSED),
        name="inproj_attn",
    )(sinks, x, xs, g, w)


def _attn_sample_kernel(q_ref, kvn_ref, ck_ref, cv_ref, sink_ref, slope_ref, o_ref):
    sp = q_ref.shape[0]
    rows_n = sp * GRP
    dist = (WINDOW - lax.broadcasted_iota(jnp.int32, (rows_n, WINDOW), 1)).astype(F32)
    r, c = _iota2((rows_n, sp * HEAD_DIM))
    own = (r >> 3) == (c >> 6)
    per_row = lambda col: jnp.concatenate([col] * sp, axis=0)
    for kv in range(N_KV):
        lo, hi = HEAD_DIM * kv, HEAD_DIM * (kv + 1)
        slope = per_row(slope_ref[GRP * kv:GRP * (kv + 1), :])
        sink = per_row(sink_ref[GRP * kv:GRP * (kv + 1), :])
        qk = q_ref[:, GRP * kv:GRP * (kv + 1), :].reshape(rows_n, HEAD_DIM)
        q_bd = jnp.where(own, jnp.concatenate([qk] * sp, axis=1), 0.0).astype(BF16)
        k_all = jnp.concatenate([ck_ref[n, :, lo:hi] for n in range(sp)], axis=1).astype(BF16)
        v_all = jnp.concatenate([cv_ref[n, :, lo:hi] for n in range(sp)], axis=1).astype(BF16)
        s_c = lax.dot_general(q_bd, k_all, (((1,), (1,)), ((), ())),
                              preferred_element_type=F32) * (HEAD_DIM ** -0.5)
        s_c = s_c - slope * dist
        kn = kvn_ref[:, :, lo:hi]
        vn = kvn_ref[:, :, KV_WIDTH + lo:KV_WIDTH + hi]
        kn_rows = jnp.broadcast_to(kn, (sp, GRP, HEAD_DIM)).reshape(rows_n, HEAD_DIM)
        s_n = jnp.sum(qk * kn_rows, axis=-1, keepdims=True) * (HEAD_DIM ** -0.5)
        m = jnp.maximum(jnp.maximum(jnp.max(s_c, axis=-1, keepdims=True), s_n), sink)
        e_c = jnp.exp(s_c - m)
        e_n = jnp.exp(s_n - m)
        denom = jnp.sum(e_c, axis=-1, keepdims=True) + e_n + jnp.exp(sink - m)
        o_all = jnp.dot(e_c.astype(BF16), v_all, preferred_element_type=F32)
        for n in range(sp):
            rows = slice(GRP * n, GRP * (n + 1))
            o = o_all[rows, HEAD_DIM * n:HEAD_DIM * (n + 1)] + e_n[rows] * vn[n]
            o_ref[n, GRP * kv:GRP * (kv + 1), :] = o / denom[rows]


def _attn_sample(l, q3, kvn, ck, cv, sink_col, slope_col):
    n = q3.shape[0]
    sp = SAMPLE_PER_STEP
    return pl.pallas_call(
        _attn_sample_kernel,
        grid=(n // sp,),
        in_specs=[
            pl.BlockSpec((sp, N_HEADS, HEAD_DIM), lambda i: (i, 0, 0)),
            pl.BlockSpec((sp, 1, 2 * KV_WIDTH), lambda i: (i, 0, 0)),
            pl.BlockSpec((None, sp, WINDOW, KV_WIDTH), lambda i: (l, i, 0, 0)),
            pl.BlockSpec((None, sp, WINDOW, KV_WIDTH), lambda i: (l, i, 0, 0)),
            pl.BlockSpec((None, N_HEADS, 1), lambda i: (l, 0, 0)),
            pl.BlockSpec((N_HEADS, 1), lambda i: (0, 0)),
        ],
        out_specs=pl.BlockSpec((sp, N_HEADS, HEAD_DIM), lambda i: (i, 0, 0)),
        out_shape=jax.ShapeDtypeStruct((n, N_HEADS, HEAD_DIM), F32),
        compiler_params=_params(("arbitrary",)),
        name="attn_sample",
    )(q3, kvn, ck, cv, sink_col, slope_col)


def _discretise(a_re, a_im, log_dt):
    dt = jnp.exp(log_dt)
    dta_re = dt * a_re
    dta_im = dt * a_im
    mag = jnp.exp(dta_re)
    return mag * jnp.cos(dta_im), mag * jnp.sin(dta_im)


def _ssm_prep_kernel(are_ref, aim_ref, ldt_ref, bre_ref, bim_ref, cre_ref, cim_ref, wg_ref,
                     ared_ref, aimd_ref, ldtd_ref,
                     wb_ref, wbd_ref, wcr_ref, wci_ref, wglu_ref, ad_ref, ar_ref):
    copy_dot = lambda a, b: jnp.dot(a.astype(BF16), b, preferred_element_type=F32)

    a_re = are_ref[...]
    a_im = aim_ref[...]
    ab_re, ab_im = _discretise(a_re, a_im, ldt_ref[...])
    ar_ref[0] = ab_re
    ar_ref[1] = ab_im
    den = a_re * a_re + a_im * a_im
    f_re = ((ab_re - 1.0) * a_re + ab_im * a_im) / den
    f_im = (ab_im * a_re - (ab_re - 1.0) * a_im) / den
    r, _ = _iota2((LANES, SLAB_ST))
    f_t = jnp.where(r == 0, f_re, jnp.where(r == 1, f_im, 0.0)).T
    f_re = f_t[:, 0:1]
    f_im = f_t[:, 1:2]

    b_re = bre_ref[...]
    b_im = bim_ref[...]
    r, c = _iota2((SSM_CG, SLAB_CH))
    tile_c = (r == (c & (SSM_CG - 1))).astype(BF16)
    r, c = _iota2((SLAB_ST, SLAB_CH))
    diag = (r >> 6) == (c >> 4)
    bbt_re = jnp.where(diag, copy_dot(f_re * b_re - f_im * b_im, tile_c), 0.0)
    bbt_im = jnp.where(diag, copy_dot(f_re * b_im + f_im * b_re, tile_c), 0.0)
    wb_re = bbt_re.T
    wb_im = bbt_im.T
    wb_ref[:, :SLAB_ST] = wb_re.astype(BF16)
    wb_ref[:, SLAB_ST:] = wb_im.astype(BF16)
    r, _ = _iota2((SLAB_CH, LANES))
    own = [(r >> 5) == k for k in range(ST_TILES)]
    fold = lambda w: sum(jnp.where(own[k], w[:, LANES * k:LANES * (k + 1)], 0.0) for k in range(ST_TILES))
    wbd_ref[:, :LANES] = fold(wb_re).astype(BF16)
    wbd_ref[:, LANES:] = fold(wb_im).astype(BF16)

    r, c = _iota2((SSM_STATE, SLAB_ST))
    tile_p = (r == (c & (SSM_STATE - 1))).astype(BF16)
    r, c = _iota2((SLAB_CH, SLAB_ST))
    diag = (r >> 4) == (c >> 6)
    wcr_ref[...] = jnp.where(diag, copy_dot(cre_ref[...], tile_p), 0.0).T.astype(BF16)
    wci_ref[...] = jnp.where(diag, copy_dot(cim_ref[...], tile_p), 0.0).T.astype(BF16)

    r, c = _iota2((2 * SSM_CG, 2 * SLAB_CH))
    tile_e = (r == (c & (SSM_CG - 1)) + SSM_CG * (c >> 8)).astype(BF16)
    r, c = _iota2((SLAB_CH, 2 * SLAB_CH))
    diag = (r >> 4) == ((c & (SLAB_CH - 1)) >> 4)
    wglu_ref[...] = jnp.where(diag, copy_dot(wg_ref[...], tile_e), 0.0).astype(BF16)

    abd_re, abd_im = _discretise(ared_ref[...], aimd_ref[...], ldtd_ref[...])
    ad_ref[0] = abd_re
    ad_ref[1] = abd_im


def _ssm_prep(a_re, a_im, log_dt, b_re, b_im, c_re, c_im, w_glu):
    gp = SSM_GROUPS * SSM_STATE
    gc = SSM_GROUPS * SSM_CG
    ldt_gp = jnp.repeat(log_dt, SSM_STATE, axis=1)
    row = lambda a: a.reshape(DEPTH, N_SLABS, 1, SLAB_ST)
    dense = lambda a: a.reshape(DEPTH, N_SLABS, SUBLANES, LANES)
    blk = lambda rows, cols: pl.BlockSpec((None, rows, cols), lambda l, s: (l, s, 0))
    rblk = pl.BlockSpec((None, None, 1, SLAB_ST), lambda l, s: (l, s, 0, 0))
    dblk = pl.BlockSpec((None, None, SUBLANES, LANES), lambda l, s: (l, s, 0, 0))
    oblk = lambda rows, cols: pl.BlockSpec((None, None, rows, cols), lambda l, s: (l, s, 0, 0))
    return pl.pallas_call(
        _ssm_prep_kernel,
        grid=(DEPTH, N_SLABS),
        in_specs=[rblk, rblk, rblk,
                  blk(SLAB_ST, SSM_CG), blk(SLAB_ST, SSM_CG),
                  blk(SLAB_CH, SSM_STATE), blk(SLAB_CH, SSM_STATE), blk(SLAB_CH, 2 * SSM_CG),
                  dblk, dblk, dblk],
        out_specs=[oblk(SLAB_CH, 2 * SLAB_ST), oblk(SLAB_CH, 2 * LANES),
                   oblk(SLAB_ST, SLAB_CH), oblk(SLAB_ST, SLAB_CH), oblk(SLAB_CH, 2 * SLAB_CH),
                   pl.BlockSpec((None, None, 2, SUBLANES, LANES), lambda l, s: (l, s, 0, 0, 0)),
                   pl.BlockSpec((None, None, 2, 1, SLAB_ST), lambda l, s: (l, s, 0, 0, 0))],
        out_shape=[jax.ShapeDtypeStruct((DEPTH, N_SLABS, SLAB_CH, 2 * SLAB_ST), BF16),
                   jax.ShapeDtypeStruct((DEPTH, N_SLABS, SLAB_CH, 2 * LANES), BF16),
                   jax.ShapeDtypeStruct((DEPTH, N_SLABS, SLAB_ST, SLAB_CH), BF16),
                   jax.ShapeDtypeStruct((DEPTH, N_SLABS, SLAB_ST, SLAB_CH), BF16),
                   jax.ShapeDtypeStruct((DEPTH, N_SLABS, SLAB_CH, 2 * SLAB_CH), BF16),
                   jax.ShapeDtypeStruct((DEPTH, N_SLABS, 2, SUBLANES, LANES), F32),
                   jax.ShapeDtypeStruct((DEPTH, N_SLABS, 2, 1, SLAB_ST), F32)],
        compiler_params=_params(("arbitrary", "arbitrary")),
        name="ssm_prep",
    )(row(a_re), row(a_im), row(ldt_gp),
      b_re.reshape(DEPTH, gp, SSM_CG), b_im.reshape(DEPTH, gp, SSM_CG),
      c_re.reshape(DEPTH, gc, SSM_STATE), c_im.reshape(DEPTH, gc, SSM_STATE),
      w_glu.reshape(DEPTH, gc, 2 * SSM_CG),
      dense(a_re), dense(a_im), dense(ldt_gp))


def _ssm_tail(s_re, s_im, u, wcr, wci, d, wglu):
    y = (jnp.dot(s_re.astype(BF16), wcr, preferred_element_type=F32)
         - jnp.dot(s_im.astype(BF16), wci, preferred_element_type=F32)
         + d * u)
    z = jnp.dot(_gelu_tanh(y).astype(BF16), wglu, preferred_element_type=F32)
    return z[:, :SLAB_CH] * jax.nn.sigmoid(z[:, SLAB_CH:])


def _ssm_prompt_kernel(*refs):
    u_refs = refs[:SLABS_PER_STEP]
    (wbd_ref, a_ref, wcr_ref, wci_ref, d_ref, wglu_ref,
     o_ref, sre_ref, sim_ref, st_ref, carry_ref) = refs[SLABS_PER_STEP:]
    nb = u_refs[0].shape[0]
    chains = [(p, b) for p in range(SLABS_PER_STEP) for b in range(nb)]

    @pl.when(pl.program_id(1) == 0)
    def _():
        carry_ref[...] = jnp.zeros_like(carry_ref)

    sub, ch = _iota2((SUBLANES, SLAB_CH))
    own = sub == (ch >> 5)
    for p, b in chains:
        lhs = jnp.where(own[None], u_refs[p][b][:, None, :], 0.0).astype(BF16)
        bu = jnp.dot(lhs.reshape(SUBLANES * SSM_CHUNK, SLAB_CH), wbd_ref[p],
                     preferred_element_type=F32)
        st_ref[p, b, 0] = bu[:, :LANES]
        st_ref[p, b, 1] = bu[:, LANES:]

    abar = [(a_ref[p, 0], a_ref[p, 1]) for p in range(SLABS_PER_STEP)]

    def step(t, carry):
        row = pl.multiple_of(t * SUBLANES, SUBLANES)
        new = []
        for i, (p, b) in enumerate(chains):
            a_re, a_im = abar[p]
            sr, si = carry[2 * i], carry[2 * i + 1]
            nr = a_re * sr - a_im * si + st_ref[p, b, 0, pl.ds(row, SUBLANES), :]
            ni = a_re * si + a_im * sr + st_ref[p, b, 1, pl.ds(row, SUBLANES), :]
            st_ref[p, b, 0, pl.ds(row, SUBLANES), :] = nr
            st_ref[p, b, 1, pl.ds(row, SUBLANES), :] = ni
            new += [nr, ni]
        return tuple(new)

    init = tuple(carry_ref[p, b, j] for p, b in chains for j in range(2))
    fin = lax.fori_loop(0, SSM_CHUNK, step, init, unroll=2)
    for i, (p, b) in enumerate(chains):
        carry_ref[p, b, 0] = fin[2 * i]
        carry_ref[p, b, 1] = fin[2 * i + 1]
        sre_ref[p, b] = fin[2 * i]
        sim_ref[p, b] = fin[2 * i + 1]

    tiles = lambda p, b, j: jnp.concatenate(
        [st_ref[p, b, j, pl.ds(k, SSM_CHUNK, stride=SUBLANES), :] for k in range(ST_TILES)], axis=1)
    for p, b in chains:
        o_ref[b, :, SLAB_CH * p:SLAB_CH * (p + 1)] = _ssm_tail(
            tiles(p, b, 0), tiles(p, b, 1), u_refs[p][b], wcr_ref[p], wci_ref[p], d_ref[p], wglu_ref[p])


def _slab_specs(l):
    w4 = lambda rows, cols: pl.BlockSpec((None, None, rows, cols), lambda s, *_: (l, s, 0, 0))
    return dict(wb=w4(SLAB_CH, 2 * SLAB_ST), wc=w4(SLAB_ST, SLAB_CH),
                d=w4(1, SLAB_CH), wglu=w4(SLAB_CH, 2 * SLAB_CH))


def _ssm_prompt(l, z3, wbd, a_dense, wcr, wci, d_row, wglu):
    nb, t, _ = z3.shape
    sps = SLABS_PER_STEP
    ucol = (Q_WIDTH + 2 * KV_WIDTH) // SLAB_CH
    u_spec = lambda p: pl.BlockSpec((nb, SSM_CHUNK, SLAB_CH), lambda s, c: (0, c, ucol + sps * s + p))
    w4 = lambda rows, cols: pl.BlockSpec((None, sps, rows, cols), lambda s, c: (l, s, 0, 0))
    st_blk = pl.BlockSpec((sps, nb, SUBLANES, LANES), lambda s, c: (s, 0, 0, 0))
    return pl.pallas_call(
        _ssm_prompt_kernel,
        grid=(N_SLABS // sps, t // SSM_CHUNK),
        in_specs=[u_spec(p) for p in range(sps)] + [
            w4(SLAB_CH, 2 * LANES),
            pl.BlockSpec((None, sps, 2, SUBLANES, LANES), lambda s, c: (l, s, 0, 0, 0)),
            w4(SLAB_ST, SLAB_CH), w4(SLAB_ST, SLAB_CH), w4(1, SLAB_CH), w4(SLAB_CH, 2 * SLAB_CH),
        ],
        out_specs=[pl.BlockSpec((nb, SSM_CHUNK, sps * SLAB_CH), lambda s, c: (0, c, s)), st_blk, st_blk],
        out_shape=[
            jax.ShapeDtypeStruct((nb, t, SSM_WIDTH), F32),
            jax.ShapeDtypeStruct((N_SLABS, nb, SUBLANES, LANES), F32),
            jax.ShapeDtypeStruct((N_SLABS, nb, SUBLANES, LANES), F32),
        ],
        scratch_shapes=[
            pltpu.VMEM((sps, nb, 2, SUBLANES * SSM_CHUNK, LANES), F32),
            pltpu.VMEM((sps, nb, 2, SUBLANES, LANES), F32),
        ],
        compiler_params=_params(("arbitrary", "arbitrary")),
        name="ssm_prompt",
    )(*([z3] * sps), wbd, a_dense, wcr, wci, d_row, wglu)


def _ssm_sample_kernel(u_ref, wb_ref, a_ref, s0r_ref, s0i_ref, wcr_ref, wci_ref, d_ref, wglu_ref,
                       o_ref, sre_ref, sim_ref):
    u = u_ref[...]
    bu = jnp.dot(u.astype(BF16), wb_ref[...], preferred_element_type=F32)
    a_re = a_ref[0]
    a_im = a_ref[1]
    s0r = s0r_ref[...]
    s0i = s0i_ref[...]
    nr = bu[:, :SLAB_ST] + a_re * s0r - a_im * s0i
    ni = bu[:, SLAB_ST:] + a_re * s0i + a_im * s0r
    sre_ref[...] = nr
    sim_ref[...] = ni
    o_ref[...] = _ssm_tail(nr, ni, u, wcr_ref[...], wci_ref[...], d_ref[...], wglu_ref[...])


def _ssm_sample(l, z, wb, a_row, s0r, s0i, wcr, wci, d_row, wglu):
    n = z.shape[0]
    ucol = (Q_WIDTH + 2 * KV_WIDTH) // SLAB_CH
    sp = _slab_specs(l)
    st_in = pl.BlockSpec((None, n, SLAB_ST), lambda s: (l, 0, s))
    st_out = pl.BlockSpec((n, SLAB_ST), lambda s: (0, s))
    return pl.pallas_call(
        _ssm_sample_kernel,
        grid=(N_SLABS,),
        in_specs=[
            pl.BlockSpec((n, SLAB_CH), lambda s: (0, ucol + s)),
            sp["wb"],
            pl.BlockSpec((None, None, 2, 1, SLAB_ST), lambda s: (l, s, 0, 0, 0)),
            st_in, st_in, sp["wc"], sp["wc"], sp["d"], sp["wglu"],
        ],
        out_specs=[pl.BlockSpec((n, SLAB_CH), lambda s: (0, s)), st_out, st_out],
        out_shape=[
            jax.ShapeDtypeStruct((n, SSM_WIDTH), F32),
            jax.ShapeDtypeStruct((n, SSM_GROUPS * SSM_STATE), F32),
            jax.ShapeDtypeStruct((n, SSM_GROUPS * SSM_STATE), F32),
        ],
        compiler_params=_params(("arbitrary",)),
        name="ssm_sample",
    )(z, wb, a_row, s0r, s0i, wcr, wci, d_row, wglu)


def _outproj_kernel(a_ref, m_ref, x_ref, as_ref, ms_ref, xs_ref, ga_ref, gs_ref, gp_ref, w_ref,
                    o_ref, os_ref, wres_ref, m0_ref, y0_ref, *, nw, nt):
    s = pl.program_id(0)

    def merge(attn, ssm):
        return jnp.concatenate([_rms(attn, ga_ref[...]).astype(BF16),
                                _rms(ssm, gs_ref[...]).astype(BF16)], axis=1)

    def tile(attn, ssm, x):
        y = jnp.dot(merge(attn, ssm), wres_ref[...], preferred_element_type=F32)
        return x + _rms(y, gp_ref[...])

    @pl.when(s == 0)
    def _():
        m0_ref[...] = merge(a_ref[...], m_ref[...])

    def tile0(cols, chunk):
        y0_ref[:, cols] = jnp.dot(m0_ref[...], chunk, preferred_element_type=F32)

    _on_weight_chunks(s, w_ref, wres_ref, nw, tile0)

    @pl.when(s == nw - 1)
    def _():
        o_ref[...] = x_ref[...] + _rms(y0_ref[...], gp_ref[...])

    @pl.when(_is_tile_step(s, nw, nt))
    def _():
        o_ref[...] = tile(a_ref[...], m_ref[...], x_ref[...])

    @pl.when(s == nw + nt - 1)
    def _():
        os_ref[...] = tile(as_ref[...], ms_ref[...], xs_ref[...])


def _outproj(l, attn, ssm, x, attn_s, ssm_s, xs, ga, gs, gp, w):
    m, ns = x.shape[0], xs.shape[0]
    nw, nt = D_MODEL // TN_W, m // TM_RES
    tile = lambda s: (_tile_index(s, nw, nt), 0)
    const = lambda s: (0, 0)
    gain = lambda n: pl.BlockSpec((None, 1, n), lambda s: (l, 0, 0))
    return pl.pallas_call(
        functools.partial(_outproj_kernel, nw=nw, nt=nt),
        grid=(nw + nt,),
        in_specs=[
            pl.BlockSpec((TM_RES, Q_WIDTH), tile),
            pl.BlockSpec((TM_RES, SSM_WIDTH), tile),
            pl.BlockSpec((TM_RES, D_MODEL), tile),
            pl.BlockSpec((ns, Q_WIDTH), const),
            pl.BlockSpec((ns, SSM_WIDTH), const),
            pl.BlockSpec((ns, D_MODEL), const),
            gain(Q_WIDTH), gain(SSM_WIDTH), gain(D_MODEL),
            pl.BlockSpec((None, D_MODEL, TN_W), lambda s: (l, 0, jnp.minimum(s, nw - 1))),
        ],
        out_specs=[pl.BlockSpec((TM_RES, D_MODEL), tile), pl.BlockSpec((ns, D_MODEL), const)],
        out_shape=[jax.ShapeDtypeStruct((m, D_MODEL), F32), jax.ShapeDtypeStruct((ns, D_MODEL), F32)],
        scratch_shapes=[pltpu.VMEM((D_MODEL, D_MODEL), BF16), pltpu.VMEM((TM_RES, D_MODEL), BF16),
                        pltpu.VMEM((TM_RES, D_MODEL), F32)],
        compiler_params=_params(("arbitrary",)),
        name="outproj",
    )(attn, ssm, x, attn_s, ssm_s, xs, ga, gs, gp, w)


def _ffn_kernel(x_ref, xs_ref, g1_ref, wg_ref, wu_ref, wd_ref, o_ref, os_ref, h_ref):
    i = pl.program_id(0)
    j = pl.program_id(1)
    tm = x_ref.shape[0]

    def swiglu_down(h):
        gate = jnp.dot(h, wg_ref[...].astype(BF16), preferred_element_type=F32)
        up = jnp.dot(h, wu_ref[...].astype(BF16), preferred_element_type=F32)
        act = (gate * jax.nn.sigmoid(gate) * up).astype(BF16)
        return jnp.dot(act, wd_ref[...].astype(BF16), preferred_element_type=F32)

    @pl.when((i == 0) & (j == 0))
    def _():
        h = jnp.concatenate([_rms(x_ref[...], g1_ref[...]).astype(BF16),
                             _rms(xs_ref[...], g1_ref[...]).astype(BF16)], axis=0)
        h_ref[...] = h
        down = swiglu_down(h)
        o_ref[...] = down[:tm]
        os_ref[...] = down[tm:]

    @pl.when((i == 0) & (j > 0))
    def _():
        down = swiglu_down(h_ref[...])
        o_ref[...] += down[:tm]
        os_ref[...] += down[tm:]

    @pl.when((i > 0) & (j == 0))
    def _():
        h = _rms(x_ref[...], g1_ref[...]).astype(BF16)
        h_ref[:tm, :] = h
        o_ref[...] = swiglu_down(h)

    @pl.when((i > 0) & (j > 0))
    def _():
        o_ref[...] += swiglu_down(h_ref[:tm, :])


def _ffn(l, x, xs, g1, wgu, wd):
    m, ns = x.shape[0], xs.shape[0]
    nf = D_FF // TF_FFN
    gain = pl.BlockSpec((None, 1, D_MODEL), lambda i, j: (l, 0, 0))
    return pl.pallas_call(
        _ffn_kernel,
        grid=(m // TM_FFN, nf),
        in_specs=[
            pl.BlockSpec((TM_FFN, D_MODEL), lambda i, j: (i, 0), pipeline_mode=pl.Buffered(1)),
            pl.BlockSpec((ns, D_MODEL), lambda i, j: (0, 0)),
            gain,
            pl.BlockSpec((None, D_MODEL, TF_FFN), lambda i, j: (l, 0, j)),
            pl.BlockSpec((None, D_MODEL, TF_FFN), lambda i, j: (l, 0, nf + j)),
            pl.BlockSpec((None, TF_FFN, D_MODEL), lambda i, j: (l, j, 0)),
        ],
        out_specs=[pl.BlockSpec((TM_FFN, D_MODEL), lambda i, j: (i, 0)),
                   pl.BlockSpec((ns, D_MODEL), lambda i, j: (0, 0))],
        out_shape=[jax.ShapeDtypeStruct((m, D_MODEL), F32), jax.ShapeDtypeStruct((ns, D_MODEL), F32)],
        scratch_shapes=[pltpu.VMEM((TM_FFN + ns, D_MODEL), BF16)],
        compiler_params=_params(("arbitrary", "arbitrary")),
        name="ffn",
    )(x, xs, g1, wgu, wgu, wd)


def _ple_kernel(x_ref, f_ref, pe_ref, xs_ref, fs_ref, pes_ref, g_ref, wg_ref, wp_ref, o_ref, os_ref,
                wgres_ref, wpres_ref, x0_ref, pe0_ref, *, nw, nt):
    s = pl.program_id(0)

    def ffn_residual(x1, f):
        return x1 + _rms(f, g_ref[...])

    def tile(x1, f, pe):
        x2 = ffn_residual(x1, f)
        gate = jnp.dot(x2.astype(BF16), wgres_ref[...], preferred_element_type=F32)
        proj = jnp.dot(pe.astype(BF16), wpres_ref[...], preferred_element_type=F32)
        return x2 + jax.nn.sigmoid(gate) * proj

    @pl.when(s == 0)
    def _():
        x2 = ffn_residual(x_ref[...], f_ref[...])
        o_ref[...] = x2
        x0_ref[...] = x2.astype(BF16)
        pe0_ref[...] = pe_ref[...].astype(BF16)

    def tile0(cols, gate_chunk):
        proj_chunk = wp_ref[...].astype(BF16)
        wpres_ref[:, cols] = proj_chunk
        gate = jnp.dot(x0_ref[...], gate_chunk, preferred_element_type=F32)
        proj = jnp.dot(pe0_ref[...], proj_chunk, preferred_element_type=F32)
        o_ref[:, cols] = o_ref[:, cols] + jax.nn.sigmoid(gate) * proj

    _on_weight_chunks(s, wg_ref, wgres_ref, nw, tile0)

    @pl.when(_is_tile_step(s, nw, nt))
    def _():
        o_ref[...] = tile(x_ref[...], f_ref[...], pe_ref[...])

    @pl.when(s == nw + nt - 1)
    def _():
        os_ref[...] = tile(xs_ref[...], fs_ref[...], pes_ref[...])


def _ple(l, x, f, pe, xs, fs, pes, g, wg, wp):
    m, ns = x.shape[0], xs.shape[0]
    nw, nt = D_MODEL // TN_W, m // TM_RES
    tile = lambda s: (_tile_index(s, nw, nt), 0)
    const = lambda s: (0, 0)
    chunk = lambda s: (l, 0, jnp.minimum(s, nw - 1))
    return pl.pallas_call(
        functools.partial(_ple_kernel, nw=nw, nt=nt),
        grid=(nw + nt,),
        in_specs=[
            pl.BlockSpec((TM_RES, D_MODEL), tile),
            pl.BlockSpec((TM_RES, D_MODEL), tile),
            pl.BlockSpec((None, TM_RES, PLE_DIM), lambda s: (l, _tile_index(s, nw, nt), 0)),
            pl.BlockSpec((ns, D_MODEL), const),
            pl.BlockSpec((ns, D_MODEL), const),
            pl.BlockSpec((None, ns, PLE_DIM), lambda s: (l, 0, 0)),
            pl.BlockSpec((None, 1, D_MODEL), lambda s: (l, 0, 0)),
            pl.BlockSpec((None, D_MODEL, TN_W), chunk),
            pl.BlockSpec((None, PLE_DIM, TN_W), chunk),
        ],
        out_specs=[pl.BlockSpec((TM_RES, D_MODEL), tile), pl.BlockSpec((ns, D_MODEL), const)],
        out_shape=[jax.ShapeDtypeStruct((m, D_MODEL), F32), jax.ShapeDtypeStruct((ns, D_MODEL), F32)],
        scratch_shapes=[pltpu.VMEM((D_MODEL, D_MODEL), BF16), pltpu.VMEM((PLE_DIM, D_MODEL), BF16),
                        pltpu.VMEM((TM_RES, D_MODEL), BF16), pltpu.VMEM((TM_RES, PLE_DIM), BF16)],
        compiler_params=_params(("arbitrary",), vmem_limit=VMEM_LIMIT_FUSED),
        name="ple",
    )(x, f, pe, xs, fs, pes, g, wg, wp)


def kernel(x_prompt, x_sample, cache_k, cache_v, state_ssm_re, state_ssm_im, p_prompt, p_sample,
           g_pre_mix, w_in, attn_sinks, ssm_a_re, ssm_a_im, ssm_log_dt, ssm_b_re, ssm_b_im,
           ssm_c_re, ssm_c_im, ssm_d, ssm_w_glu, g_attn_out, g_ssm_out, w_out, g_post_mix,
           g_pre_ffn, w_gate_up, w_down, g_post_ffn, w_ple_gate, w_ple_proj):
    nb, t, _ = x_prompt.shape
    ns = x_sample.shape[0]

    wb, wbd, wcr, wci, wglu, a_dense, a_row = _ssm_prep(ssm_a_re, ssm_a_im, ssm_log_dt, ssm_b_re, ssm_b_im,
                                                        ssm_c_re, ssm_c_im, ssm_w_glu)
    d_row = ssm_d.reshape(DEPTH, N_SLABS, 1, SLAB_CH)
    slope_col = jnp.asarray(_SLOPES, F32).reshape(N_HEADS, 1)
    sink_col = attn_sinks.reshape(DEPTH, N_HEADS, 1)
    gain = lambda g: g.reshape(DEPTH, 1, g.shape[-1])
    g_pre_mix, g_attn_out, g_ssm_out, g_post_mix, g_pre_ffn, g_post_ffn = map(
        gain, (g_pre_mix, g_attn_out, g_ssm_out, g_post_mix, g_pre_ffn, g_post_ffn))

    xp = x_prompt.reshape(nb * t, D_MODEL)
    xs = x_sample.reshape(ns, D_MODEL)
    pp = p_prompt.reshape(DEPTH, nb * t, PLE_DIM)
    ps = p_sample.reshape(DEPTH, ns, PLE_DIM)
    s0r = state_ssm_re.reshape(DEPTH, ns, SSM_GROUPS * SSM_STATE)
    s0i = state_ssm_im.reshape(DEPTH, ns, SSM_GROUPS * SSM_STATE)
    ck = cache_k.reshape(DEPTH, ns, WINDOW, KV_WIDTH)
    cv = cache_v.reshape(DEPTH, ns, WINDOW, KV_WIDTH)

    kv_lo, kv_hi = Q_WIDTH, Q_WIDTH + 2 * KV_WIDTH

    kvp_l, srp_l, sip_l, kvs_l, srs_l, sis_l = [], [], [], [], [], []
    for l in range(DEPTH):
        z, zs, attn = _inproj_attn(l, xp, xs, g_pre_mix, w_in, attn_sinks, t)
        z3 = z.reshape(nb, t, IN_WIDTH)

        ssm, sre, sim = _ssm_prompt(l, z3, wbd, a_dense, wcr, wci, d_row, wglu)

        kvn = zs[:, kv_lo:kv_hi].reshape(ns, 1, 2 * KV_WIDTH)
        attn_s = _attn_sample(l, zs[:, :Q_WIDTH].reshape(ns, N_HEADS, HEAD_DIM), kvn,
                              ck, cv, sink_col, slope_col)
        ssm_s, srs, sis = _ssm_sample(l, zs, wb, a_row, s0r, s0i, wcr, wci, d_row, wglu)

        x1, x1s = _outproj(l, attn, ssm.reshape(nb * t, SSM_WIDTH), xp,
                           attn_s.reshape(ns, Q_WIDTH), ssm_s, xs,
                           g_attn_out, g_ssm_out, g_post_mix, w_out)
        f, fs = _ffn(l, x1, x1s, g_pre_ffn, w_gate_up, w_down)
        xp, xs = _ple(l, x1, f, pp, x1s, fs, ps, g_post_ffn, w_ple_gate, w_ple_proj)

        kvp_l.append(z3[:, t - WINDOW:, kv_lo:kv_hi])
        kvs_l.append(kvn)
        srp_l.append(sre)
        sip_l.append(sim)
        srs_l.append(srs)
        sis_l.append(sis)

    heads = lambda a: a.reshape(a.shape[:-1] + (N_KV, HEAD_DIM))
    kvp = jnp.stack(kvp_l)
    kvs = jnp.stack(kvs_l)
    new_ks = jnp.concatenate([ck[:, :, 1:], kvs[..., :KV_WIDTH]], axis=2)
    new_vs = jnp.concatenate([cv[:, :, 1:], kvs[..., KV_WIDTH:]], axis=2)
    unslab = lambda s: jnp.swapaxes(jnp.stack(s), 1, 2).reshape(DEPTH, nb, SSM_GROUPS, SSM_STATE)
    states = lambda s: jnp.stack(s).reshape(DEPTH, ns, SSM_GROUPS, SSM_STATE)
    return (xp.reshape(nb, t, D_MODEL), xs.reshape(ns, 1, D_MODEL),
            heads(kvp[..., :KV_WIDTH]), heads(kvp[..., KV_WIDTH:]), unslab(srp_l), unslab(sip_l),
            heads(new_ks), heads(new_vs), states(srs_l), states(sis_l))
```

```python
import functools
import math

import numpy as np
import jax
import jax.numpy as jnp
from jax import lax
from jax.experimental import pallas as pl
from jax.experimental.pallas import tpu as pltpu

F32 = jnp.float32
BF16 = jnp.bfloat16

D_MODEL = 2048
DEPTH = 4
HEAD_DIM = 64
N_HEADS = 16
N_KV = 2
GRP = N_HEADS // N_KV
WINDOW = 128
Q_WIDTH = N_HEADS * HEAD_DIM
KV_WIDTH = N_KV * HEAD_DIM
SSM_WIDTH = 1024
SSM_CG = 16
SSM_GROUPS = 64
SSM_STATE = 64
IN_WIDTH = Q_WIDTH + 2 * KV_WIDTH + SSM_WIDTH
D_FF = 5632
PLE_DIM = 256
EPS = 1e-6
NEG_INF = -1e30

LANES = 128
SUBLANES = 8

SLAB_GROUPS = 16
N_SLABS = SSM_GROUPS // SLAB_GROUPS
SLAB_CH = SLAB_GROUPS * SSM_CG
SLAB_ST = SLAB_GROUPS * SSM_STATE
ST_TILES = SLAB_ST // LANES
SSM_CHUNK = 256
SLABS_PER_STEP = 2

TM_RES = 512
TN_W = 256
TM_FFN = 1024
TF_FFN = 256
SAMPLE_PER_STEP = 8

VMEM_LIMIT = 56 * 1024 * 1024
VMEM_LIMIT_FUSED = 60 * 1024 * 1024

_SLOPES = [float(v) for v in
           (2.0 ** (-8.0 * np.arange(1, N_HEADS + 1, dtype=np.float32) / N_HEADS)).astype(np.float32)]


def _rms(x, g):
    var = jnp.mean(x * x, axis=-1, keepdims=True)
    return x * lax.rsqrt(var + EPS) * g


def _iota2(shape):
    return (lax.broadcasted_iota(jnp.int32, shape, 0), lax.broadcasted_iota(jnp.int32, shape, 1))


def _gelu_tanh(x):
    c = math.sqrt(2.0 / math.pi)
    return 0.5 * x * (1.0 + jnp.tanh(c * (x + 0.044715 * (x * x * x))))


def _params(sem, vmem_limit=VMEM_LIMIT):
    return pltpu.CompilerParams(dimension_semantics=sem, vmem_limit_bytes=vmem_limit)


def _tile_index(s, nw, nt):
    return jnp.minimum(jnp.maximum(s - nw + 1, 0), nt - 1)


def _is_tile_step(s, nw, nt):
    return (s >= nw) & (s < nw + nt - 1)


def _on_weight_chunks(s, w_ref, wres_ref, nw, apply):
    tn = w_ref.shape[1]
    for c in range(nw):
        @pl.when(s == c)
        def _(c=c):
            chunk = w_ref[...].astype(BF16)
            wres_ref[:, c * tn:(c + 1) * tn] = chunk
            apply(slice(c * tn, (c + 1) * tn), chunk)


def _attend_tile(tau, l, tiles_per_seq, sink_ref, bias_ref, zq_ref, zkv_ref, o_ref):
    tau = jnp.asarray(tau, jnp.int32)
    qslot = lax.rem(tau, 2)
    cur = lax.rem(tau, 3)
    prev = lax.rem(tau + 2, 3)
    seq_start = lax.rem(tau, tiles_per_seq) == 0
    lane = lax.broadcasted_iota(jnp.int32, (1, 2 * WINDOW), 1)
    hide_prev = jnp.where(seq_start & (lane < WINDOW), NEG_INF, 0.0)
    for bq in range(TM_RES // WINDOW):
        rows = slice(WINDOW * bq, WINDOW * (bq + 1))
        if bq == 0:
            kv_prev = zkv_ref[prev, TM_RES - WINDOW:TM_RES, :]
        else:
            kv_prev = zkv_ref[cur, WINDOW * (bq - 1):WINDOW * bq, :]
        kvcat = jnp.concatenate([kv_prev, zkv_ref[cur, rows, :]], axis=0)
        q = zq_ref[qslot, rows, :]
        for kv in range(N_KV):
            heads = range(GRP * kv, GRP * (kv + 1))
            kh = kvcat[:, HEAD_DIM * kv:HEAD_DIM * (kv + 1)]
            vh = kvcat[:, KV_WIDTH + HEAD_DIM * kv:KV_WIDTH + HEAD_DIM * (kv + 1)]
            qg = jnp.concatenate([q[:, HEAD_DIM * h:HEAD_DIM * (h + 1)] for h in heads], axis=0)
            s_all = lax.dot_general(qg, kh, (((1,), (1,)), ((), ())), preferred_element_type=F32)
            e_all, denoms = [], []
            for i, h in enumerate(heads):
                sink = sink_ref[l, h]
                s = s_all[WINDOW * i:WINDOW * (i + 1)] + bias_ref[h]
                if bq == 0:
                    s = s + hide_prev
                m = jnp.maximum(jnp.max(s, axis=-1, keepdims=True), sink)
                e = jnp.exp(s - m)
                denoms.append(jnp.sum(e, axis=-1, keepdims=True) + jnp.exp(sink - m))
                e_all.append(e.astype(BF16))
            o_all = jnp.dot(jnp.concatenate(e_all, axis=0), vh, preferred_element_type=F32)
            for i, h in enumerate(heads):
                o_ref[rows, HEAD_DIM * h:HEAD_DIM * (h + 1)] = o_all[WINDOW * i:WINDOW * (i + 1)] / denoms[i]


def _inproj_attn_kernel(sink_ref, x_ref, xs_ref, g_ref, w_ref, o_ref, os_ref, oa_ref,
                        wres_ref, h0_ref, zq_ref, zkv_ref, bias_ref, *, l, nw, nt, tiles_per_seq):
    s = pl.program_id(0)
    kv_cols = slice(Q_WIDTH, Q_WIDTH + 2 * KV_WIDTH)

    @pl.when(s == 0)
    def _():
        h0_ref[...] = _rms(x_ref[...], g_ref[...]).astype(BF16)
        zkv_ref[...] = jnp.zeros_like(zkv_ref)
        qi, kj = _iota2((WINDOW, 2 * WINDOW))
        dist_i = qi - kj + WINDOW
        band = (dist_i >= 0) & (dist_i <= WINDOW)
        dist = dist_i.astype(F32)
        for h in range(N_HEADS):
            bias_ref[h] = jnp.where(band, -(_SLOPES[h] * dist), NEG_INF)

    def tile0(cols, chunk):
        o_ref[:, cols] = jnp.dot(h0_ref[...], chunk, preferred_element_type=F32)

    _on_weight_chunks(s, w_ref, wres_ref, nw, tile0)

    def keep_for_attention(t):
        zq_ref[lax.rem(t, 2)] = (o_ref[:, :Q_WIDTH] * (HEAD_DIM ** -0.5)).astype(BF16)
        zkv_ref[lax.rem(t, 3)] = o_ref[:, kv_cols].astype(BF16)

    @pl.when(s == nw - 1)
    def _():
        keep_for_attention(jnp.int32(0))

    @pl.when(_is_tile_step(s, nw, nt))
    def _():
        t = s - nw + 1
        _attend_tile(t - 1, l, tiles_per_seq, sink_ref, bias_ref, zq_ref, zkv_ref, oa_ref)
        h = _rms(x_ref[...], g_ref[...]).astype(BF16)
        o_ref[...] = jnp.dot(h, wres_ref[...], preferred_element_type=F32)
        keep_for_attention(t)

    @pl.when(s == nw + nt - 1)
    def _():
        _attend_tile(nt - 1, l, tiles_per_seq, sink_ref, bias_ref, zq_ref, zkv_ref, oa_ref)
        h = _rms(xs_ref[...], g_ref[...]).astype(BF16)
        os_ref[...] = jnp.dot(h, wres_ref[...], preferred_element_type=F32)


def _inproj_attn(l, x, xs, g, w, sinks, t):
    m, ns = x.shape[0], xs.shape[0]
    n = w.shape[2]
    nw, nt = n // TN_W, m // TM_RES
    tile = lambda s: (_tile_index(s, nw, nt), 0)
    attn_tile = lambda s: (jnp.minimum(jnp.maximum(s - nw, 0), nt - 1), 0)
    const = lambda s: (0, 0)
    return pl.pallas_call(
        functools.partial(_inproj_attn_kernel, l=l, nw=nw, nt=nt, tiles_per_seq=t // TM_RES),
        grid=(nw + nt,),
        in_specs=[
            pl.BlockSpec(memory_space=pltpu.SMEM),
            pl.BlockSpec((TM_RES, D_MODEL), tile),
            pl.BlockSpec((ns, D_MODEL), const),
            pl.BlockSpec((None, 1, D_MODEL), lambda s: (l, 0, 0)),
            pl.BlockSpec((None, D_MODEL, TN_W), lambda s: (l, 0, jnp.minimum(s, nw - 1))),
        ],
        out_specs=[pl.BlockSpec((TM_RES, n), tile), pl.BlockSpec((ns, n), const),
                   pl.BlockSpec((TM_RES, Q_WIDTH), attn_tile)],
        out_shape=[jax.ShapeDtypeStruct((m, n), F32), jax.ShapeDtypeStruct((ns, n), F32),
                   jax.ShapeDtypeStruct((m, Q_WIDTH), F32)],
        scratch_shapes=[pltpu.VMEM((D_MODEL, n), BF16), pltpu.VMEM((TM_RES, D_MODEL), BF16),
                        pltpu.VMEM((2, TM_RES, Q_WIDTH), BF16), pltpu.VMEM((3, TM_RES, 2 * KV_WIDTH), BF16),
                        pltpu.VMEM((N_HEADS, WINDOW, 2 * WINDOW), F32)],
        compiler_params=_params(("arbitrary",), vmem_limit=VMEM_LIMIT_FUSED),
        name="inproj_attn",
    )(sinks, x, xs, g, w)


def _attn_sample_kernel(q_ref, kvn_ref, ck_ref, cv_ref, sink_ref, slope_ref, o_ref, nk_ref, nv_ref):
    sp = q_ref.shape[0]
    rows_n = sp * GRP
    dist = (WINDOW - lax.broadcasted_iota(jnp.int32, (rows_n, WINDOW), 1)).astype(F32)
    r, c = _iota2((rows_n, sp * HEAD_DIM))
    own = (r >> (GRP.bit_length() - 1)) == (c >> (HEAD_DIM.bit_length() - 1))
    per_row = lambda col: jnp.concatenate([col] * sp, axis=0)
    for kv in range(N_KV):
        lo, hi = HEAD_DIM * kv, HEAD_DIM * (kv + 1)
        slope = per_row(slope_ref[GRP * kv:GRP * (kv + 1), :])
        sink = per_row(sink_ref[GRP * kv:GRP * (kv + 1), :])
        qk = q_ref[:, GRP * kv:GRP * (kv + 1), :].reshape(rows_n, HEAD_DIM)
        q_bd = jnp.where(own, jnp.concatenate([qk] * sp, axis=1), 0.0).astype(BF16)
        k_all = jnp.concatenate([ck_ref[n, :, lo:hi] for n in range(sp)], axis=1).astype(BF16)
        v_all = jnp.concatenate([cv_ref[n, :, lo:hi] for n in range(sp)], axis=1).astype(BF16)
        s_c = lax.dot_general(q_bd, k_all, (((1,), (1,)), ((), ())),
                              preferred_element_type=F32) * (HEAD_DIM ** -0.5)
        s_c = s_c - slope * dist
        kn = kvn_ref[:, :, lo:hi]
        vn = kvn_ref[:, :, KV_WIDTH + lo:KV_WIDTH + hi]
        kn_rows = jnp.broadcast_to(kn, (sp, GRP, HEAD_DIM)).reshape(rows_n, HEAD_DIM)
        s_n = jnp.sum(qk * kn_rows, axis=-1, keepdims=True) * (HEAD_DIM ** -0.5)
        m = jnp.maximum(jnp.maximum(jnp.max(s_c, axis=-1, keepdims=True), s_n), sink)
        e_c = jnp.exp(s_c - m)
        e_n = jnp.exp(s_n - m)
        denom = jnp.sum(e_c, axis=-1, keepdims=True) + e_n + jnp.exp(sink - m)
        o_all = jnp.dot(e_c.astype(BF16), v_all, preferred_element_type=F32)
        for n in range(sp):
            rows = slice(GRP * n, GRP * (n + 1))
            o = o_all[rows, HEAD_DIM * n:HEAD_DIM * (n + 1)] + e_n[rows] * vn[n]
            o_ref[n, GRP * kv:GRP * (kv + 1), :] = o / denom[rows]

    for n in range(sp):
        nk_ref[n, 0:WINDOW - 1, :] = ck_ref[n, 1:WINDOW, :]
        nk_ref[n, WINDOW - 1:WINDOW, :] = kvn_ref[n, :, :KV_WIDTH]
        nv_ref[n, 0:WINDOW - 1, :] = cv_ref[n, 1:WINDOW, :]
        nv_ref[n, WINDOW - 1:WINDOW, :] = kvn_ref[n, :, KV_WIDTH:]


def _attn_sample(l, q3, kvn, ck, cv, sink_col, slope_col):
    n = q3.shape[0]
    sp = SAMPLE_PER_STEP
    return pl.pallas_call(
        _attn_sample_kernel,
        grid=(n // sp,),
        in_specs=[
            pl.BlockSpec((sp, N_HEADS, HEAD_DIM), lambda i: (i, 0, 0)),
            pl.BlockSpec((sp, 1, 2 * KV_WIDTH), lambda i: (i, 0, 0)),
            pl.BlockSpec((None, sp, WINDOW, KV_WIDTH), lambda i: (l, i, 0, 0)),
            pl.BlockSpec((None, sp, WINDOW, KV_WIDTH), lambda i: (l, i, 0, 0)),
            pl.BlockSpec((None, N_HEADS, 1), lambda i: (l, 0, 0)),
            pl.BlockSpec((N_HEADS, 1), lambda i: (0, 0)),
        ],
        out_specs=[pl.BlockSpec((sp, N_HEADS, HEAD_DIM), lambda i: (i, 0, 0)),
                   pl.BlockSpec((sp, WINDOW, KV_WIDTH), lambda i: (i, 0, 0)),
                   pl.BlockSpec((sp, WINDOW, KV_WIDTH), lambda i: (i, 0, 0))],
        out_shape=[jax.ShapeDtypeStruct((n, N_HEADS, HEAD_DIM), F32),
                   jax.ShapeDtypeStruct((n, WINDOW, KV_WIDTH), F32),
                   jax.ShapeDtypeStruct((n, WINDOW, KV_WIDTH), F32)],
        compiler_params=_params(("arbitrary",)),
        name="attn_sample",
    )(q3, kvn, ck, cv, sink_col, slope_col)


def _discretise(a_re, a_im, log_dt):
    dt = jnp.exp(log_dt)
    dta_re = dt * a_re
    dta_im = dt * a_im
    mag = jnp.exp(dta_re)
    return mag * jnp.cos(dta_im), mag * jnp.sin(dta_im)


def _ssm_prep_kernel(are_ref, aim_ref, ldt_ref, bre_ref, bim_ref, cre_ref, cim_ref, wg_ref,
                     ared_ref, aimd_ref, ldtd_ref,
                     wb_ref, wbd_ref, wcr_ref, wci_ref, wglu_ref, ad_ref, ar_ref):
    copy_dot = lambda a, b: jnp.dot(a.astype(BF16), b, preferred_element_type=F32)

    a_re = are_ref[...]
    a_im = aim_ref[...]
    ab_re, ab_im = _discretise(a_re, a_im, ldt_ref[...])
    ar_ref[0] = ab_re
    ar_ref[1] = ab_im
    den = a_re * a_re + a_im * a_im
    f_re = ((ab_re - 1.0) * a_re + ab_im * a_im) / den
    f_im = (ab_im * a_re - (ab_re - 1.0) * a_im) / den
    r, _ = _iota2((LANES, SLAB_ST))
    f_t = jnp.where(r == 0, f_re, jnp.where(r == 1, f_im, 0.0)).T
    f_re = f_t[:, 0:1]
    f_im = f_t[:, 1:2]

    b_re = bre_ref[...].reshape(SLAB_ST, SSM_CG)
    b_im = bim_ref[...].reshape(SLAB_ST, SSM_CG)
    c_re = cre_ref[...].reshape(SLAB_CH, SSM_STATE)
    c_im = cim_ref[...].reshape(SLAB_CH, SSM_STATE)
    w_glu = wg_ref[...].reshape(SLAB_CH, 2 * SSM_CG)
    r, c = _iota2((SSM_CG, SLAB_CH))
    tile_c = (r == (c & (SSM_CG - 1))).astype(BF16)
    r, c = _iota2((SLAB_ST, SLAB_CH))
    diag = (r >> 6) == (c >> 4)
    bbt_re = jnp.where(diag, copy_dot(f_re * b_re - f_im * b_im, tile_c), 0.0)
    bbt_im = jnp.where(diag, copy_dot(f_re * b_im + f_im * b_re, tile_c), 0.0)
    wb_re = bbt_re.T
    wb_im = bbt_im.T
    wb_ref[:, :SLAB_ST] = wb_re.astype(BF16)
    wb_ref[:, SLAB_ST:] = wb_im.astype(BF16)
    r, _ = _iota2((SLAB_CH, LANES))
    own = [(r >> 5) == k for k in range(ST_TILES)]
    fold = lambda w: sum(jnp.where(own[k], w[:, LANES * k:LANES * (k + 1)], 0.0) for k in range(ST_TILES))
    wbd_ref[:, :LANES] = fold(wb_re).astype(BF16)
    wbd_ref[:, LANES:] = fold(wb_im).astype(BF16)

    r, c = _iota2((SSM_STATE, SLAB_ST))
    tile_p = (r == (c & (SSM_STATE - 1))).astype(BF16)
    r, c = _iota2((SLAB_CH, SLAB_ST))
    diag = (r >> 4) == (c >> 6)
    wcr_ref[...] = jnp.where(diag, copy_dot(c_re, tile_p), 0.0).T.astype(BF16)
    wci_ref[...] = jnp.where(diag, copy_dot(c_im, tile_p), 0.0).T.astype(BF16)

    r, c = _iota2((2 * SSM_CG, 2 * SLAB_CH))
    tile_e = (r == (c & (SSM_CG - 1)) + SSM_CG * (c >> 8)).astype(BF16)
    r, c = _iota2((SLAB_CH, 2 * SLAB_CH))
    diag = (r >> 4) == ((c & (SLAB_CH - 1)) >> 4)
    wglu_ref[...] = jnp.where(diag, copy_dot(w_glu, tile_e), 0.0).astype(BF16)

    abd_re, abd_im = _discretise(ared_ref[...], aimd_ref[...], ldtd_ref[...])
    ad_ref[0] = abd_re
    ad_ref[1] = abd_im


def _ssm_prep(a_re, a_im, log_dt, b_re, b_im, c_re, c_im, w_glu):
    ldt_gp = jnp.repeat(log_dt, SSM_STATE, axis=1)
    row = lambda a: a.reshape(DEPTH, N_SLABS, 1, SLAB_ST)
    dense = lambda a: a.reshape(DEPTH, N_SLABS, SUBLANES, LANES)
    blk = lambda rows, cols: pl.BlockSpec((None, SLAB_GROUPS, rows, cols), lambda l, s: (l, s, 0, 0))
    rblk = pl.BlockSpec((None, None, 1, SLAB_ST), lambda l, s: (l, s, 0, 0))
    dblk = pl.BlockSpec((None, None, SUBLANES, LANES), lambda l, s: (l, s, 0, 0))
    oblk = lambda rows, cols: pl.BlockSpec((None, None, rows, cols), lambda l, s: (l, s, 0, 0))
    return pl.pallas_call(
        _ssm_prep_kernel,
        grid=(DEPTH, N_SLABS),
        in_specs=[rblk, rblk, rblk,
                  blk(SSM_STATE, SSM_CG), blk(SSM_STATE, SSM_CG),
                  blk(SSM_CG, SSM_STATE), blk(SSM_CG, SSM_STATE), blk(SSM_CG, 2 * SSM_CG),
                  dblk, dblk, dblk],
        out_specs=[oblk(SLAB_CH, 2 * SLAB_ST), oblk(SLAB_CH, 2 * LANES),
                   oblk(SLAB_ST, SLAB_CH), oblk(SLAB_ST, SLAB_CH), oblk(SLAB_CH, 2 * SLAB_CH),
                   pl.BlockSpec((None, None, 2, SUBLANES, LANES), lambda l, s: (l, s, 0, 0, 0)),
                   pl.BlockSpec((None, None, 2, 1, SLAB_ST), lambda l, s: (l, s, 0, 0, 0))],
        out_shape=[jax.ShapeDtypeStruct((DEPTH, N_SLABS, SLAB_CH, 2 * SLAB_ST), BF16),
                   jax.ShapeDtypeStruct((DEPTH, N_SLABS, SLAB_CH, 2 * LANES), BF16),
                   jax.ShapeDtypeStruct((DEPTH, N_SLABS, SLAB_ST, SLAB_CH), BF16),
                   jax.ShapeDtypeStruct((DEPTH, N_SLABS, SLAB_ST, SLAB_CH), BF16),
                   jax.ShapeDtypeStruct((DEPTH, N_SLABS, SLAB_CH, 2 * SLAB_CH), BF16),
                   jax.ShapeDtypeStruct((DEPTH, N_SLABS, 2, SUBLANES, LANES), F32),
                   jax.ShapeDtypeStruct((DEPTH, N_SLABS, 2, 1, SLAB_ST), F32)],
        compiler_params=_params(("arbitrary", "arbitrary")),
        name="ssm_prep",
    )(row(a_re), row(a_im), row(ldt_gp), b_re, b_im, c_re, c_im, w_glu,
      dense(a_re), dense(a_im), dense(ldt_gp))


def _ssm_tail(s_re, s_im, u, wcr, wci, d, wglu):
    y = (jnp.dot(s_re.astype(BF16), wcr, preferred_element_type=F32)
         - jnp.dot(s_im.astype(BF16), wci, preferred_element_type=F32)
         + d * u)
    z = jnp.dot(_gelu_tanh(y).astype(BF16), wglu, preferred_element_type=F32)
    return z[:, :SLAB_CH] * jax.nn.sigmoid(z[:, SLAB_CH:])


def _ssm_prompt_kernel(*refs):
    u_refs = refs[:SLABS_PER_STEP]
    (wbd_ref, a_ref, wcr_ref, wci_ref, d_ref, wglu_ref,
     o_ref, sre_ref, sim_ref, st_ref, carry_ref) = refs[SLABS_PER_STEP:]
    nb = u_refs[0].shape[0]
    chains = [(p, b) for p in range(SLABS_PER_STEP) for b in range(nb)]

    @pl.when(pl.program_id(1) == 0)
    def _():
        carry_ref[...] = jnp.zeros_like(carry_ref)

    sub, ch = _iota2((SUBLANES, SLAB_CH))
    own = sub == (ch >> 5)
    for p, b in chains:
        lhs = jnp.where(own[None], u_refs[p][b][:, None, :], 0.0).astype(BF16)
        bu = jnp.dot(lhs.reshape(SUBLANES * SSM_CHUNK, SLAB_CH), wbd_ref[p],
                     preferred_element_type=F32)
        st_ref[p, b, 0] = bu[:, :LANES]
        st_ref[p, b, 1] = bu[:, LANES:]

    abar = [(a_ref[p, 0], a_ref[p, 1]) for p in range(SLABS_PER_STEP)]

    def step(t, carry):
        row = pl.multiple_of(t * SUBLANES, SUBLANES)
        new = []
        for i, (p, b) in enumerate(chains):
            a_re, a_im = abar[p]
            sr, si = carry[2 * i], carry[2 * i + 1]
            nr = a_re * sr - a_im * si + st_ref[p, b, 0, pl.ds(row, SUBLANES), :]
            ni = a_re * si + a_im * sr + st_ref[p, b, 1, pl.ds(row, SUBLANES), :]
            st_ref[p, b, 0, pl.ds(row, SUBLANES), :] = nr
            st_ref[p, b, 1, pl.ds(row, SUBLANES), :] = ni
            new += [nr, ni]
        return tuple(new)

    init = tuple(carry_ref[p, b, j] for p, b in chains for j in range(2))
    fin = lax.fori_loop(0, SSM_CHUNK, step, init, unroll=4)
    for i, (p, b) in enumerate(chains):
        carry_ref[p, b, 0] = fin[2 * i]
        carry_ref[p, b, 1] = fin[2 * i + 1]
        sre_ref[p, b] = fin[2 * i]
        sim_ref[p, b] = fin[2 * i + 1]

    tiles = lambda p, b, j: jnp.concatenate(
        [st_ref[p, b, j, pl.ds(k, SSM_CHUNK, stride=SUBLANES), :] for k in range(ST_TILES)], axis=1)
    for p, b in chains:
        o_ref[b, :, SLAB_CH * p:SLAB_CH * (p + 1)] = _ssm_tail(
            tiles(p, b, 0), tiles(p, b, 1), u_refs[p][b], wcr_ref[p], wci_ref[p], d_ref[p], wglu_ref[p])


def _slab_specs(l):
    w4 = lambda rows, cols: pl.BlockSpec((None, None, rows, cols), lambda s, *_: (l, s, 0, 0))
    return dict(wb=w4(SLAB_CH, 2 * SLAB_ST), wc=w4(SLAB_ST, SLAB_CH),
                d=w4(1, SLAB_CH), wglu=w4(SLAB_CH, 2 * SLAB_CH))


def _ssm_prompt(l, z3, wbd, a_dense, wcr, wci, d_row, wglu):
    nb, t, _ = z3.shape
    sps = SLABS_PER_STEP
    ucol = (Q_WIDTH + 2 * KV_WIDTH) // SLAB_CH
    u_spec = lambda p: pl.BlockSpec((nb, SSM_CHUNK, SLAB_CH), lambda s, c: (0, c, ucol + sps * s + p))
    w4 = lambda rows, cols: pl.BlockSpec((None, sps, rows, cols), lambda s, c: (l, s, 0, 0))
    st_blk = pl.BlockSpec((sps, nb, SUBLANES, LANES), lambda s, c: (s, 0, 0, 0))
    return pl.pallas_call(
        _ssm_prompt_kernel,
        grid=(N_SLABS // sps, t // SSM_CHUNK),
        in_specs=[u_spec(p) for p in range(sps)] + [
            w4(SLAB_CH, 2 * LANES),
            pl.BlockSpec((None, sps, 2, SUBLANES, LANES), lambda s, c: (l, s, 0, 0, 0)),
            w4(SLAB_ST, SLAB_CH), w4(SLAB_ST, SLAB_CH), w4(1, SLAB_CH), w4(SLAB_CH, 2 * SLAB_CH),
        ],
        out_specs=[pl.BlockSpec((nb, SSM_CHUNK, sps * SLAB_CH), lambda s, c: (0, c, s)), st_blk, st_blk],
        out_shape=[
            jax.ShapeDtypeStruct((nb, t, SSM_WIDTH), F32),
            jax.ShapeDtypeStruct((N_SLABS, nb, SUBLANES, LANES), F32),
            jax.ShapeDtypeStruct((N_SLABS, nb, SUBLANES, LANES), F32),
        ],
        scratch_shapes=[
            pltpu.VMEM((sps, nb, 2, SUBLANES * SSM_CHUNK, LANES), F32),
            pltpu.VMEM((sps, nb, 2, SUBLANES, LANES), F32),
        ],
        compiler_params=_params(("arbitrary", "arbitrary")),
        name="ssm_prompt",
    )(*([z3] * sps), wbd, a_dense, wcr, wci, d_row, wglu)


def _ssm_sample_kernel(u_ref, wb_ref, a_ref, s0r_ref, s0i_ref, wcr_ref, wci_ref, d_ref, wglu_ref,
                       o_ref, sre_ref, sim_ref):
    u = u_ref[...]
    bu = jnp.dot(u.astype(BF16), wb_ref[...], preferred_element_type=F32)
    a_re = a_ref[0]
    a_im = a_ref[1]
    s0r = s0r_ref[...]
    s0i = s0i_ref[...]
    nr = bu[:, :SLAB_ST] + a_re * s0r - a_im * s0i
    ni = bu[:, SLAB_ST:] + a_re * s0i + a_im * s0r
    sre_ref[...] = nr
    sim_ref[...] = ni
    o_ref[...] = _ssm_tail(nr, ni, u, wcr_ref[...], wci_ref[...], d_ref[...], wglu_ref[...])


def _ssm_sample(l, z, wb, a_row, s0r, s0i, wcr, wci, d_row, wglu):
    n = z.shape[0]
    ucol = (Q_WIDTH + 2 * KV_WIDTH) // SLAB_CH
    sp = _slab_specs(l)
    st_in = pl.BlockSpec((None, n, SLAB_ST), lambda s: (l, 0, s))
    st_out = pl.BlockSpec((n, SLAB_ST), lambda s: (0, s))
    return pl.pallas_call(
        _ssm_sample_kernel,
        grid=(N_SLABS,),
        in_specs=[
            pl.BlockSpec((n, SLAB_CH), lambda s: (0, ucol + s)),
            sp["wb"],
            pl.BlockSpec((None, None, 2, 1, SLAB_ST), lambda s: (l, s, 0, 0, 0)),
            st_in, st_in, sp["wc"], sp["wc"], sp["d"], sp["wglu"],
        ],
        out_specs=[pl.BlockSpec((n, SLAB_CH), lambda s: (0, s)), st_out, st_out],
        out_shape=[
            jax.ShapeDtypeStruct((n, SSM_WIDTH), F32),
            jax.ShapeDtypeStruct((n, SSM_GROUPS * SSM_STATE), F32),
            jax.ShapeDtypeStruct((n, SSM_GROUPS * SSM_STATE), F32),
        ],
        compiler_params=_params(("arbitrary",)),
        name="ssm_sample",
    )(z, wb, a_row, s0r, s0i, wcr, wci, d_row, wglu)


def _outproj_kernel(a_ref, m_ref, x_ref, as_ref, ms_ref, xs_ref, ga_ref, gs_ref, gp_ref, w_ref,
                    o_ref, os_ref, wres_ref, m0_ref, y0_ref, *, nw, nt):
    s = pl.program_id(0)

    def merge(attn, ssm):
        return jnp.concatenate([_rms(attn, ga_ref[...]).astype(BF16),
                                _rms(ssm, gs_ref[...]).astype(BF16)], axis=1)

    def tile(attn, ssm, x):
        y = jnp.dot(merge(attn, ssm), wres_ref[...], preferred_element_type=F32)
        return x + _rms(y, gp_ref[...])

    @pl.when(s == 0)
    def _():
        m0_ref[...] = merge(a_ref[...], m_ref[...])

    def tile0(cols, chunk):
        y0_ref[:, cols] = jnp.dot(m0_ref[...], chunk, preferred_element_type=F32)

    _on_weight_chunks(s, w_ref, wres_ref, nw, tile0)

    @pl.when(s == nw - 1)
    def _():
        o_ref[...] = x_ref[...] + _rms(y0_ref[...], gp_ref[...])

    @pl.when(_is_tile_step(s, nw, nt))
    def _():
        o_ref[...] = tile(a_ref[...], m_ref[...], x_ref[...])

    @pl.when(s == nw + nt - 1)
    def _():
        os_ref[...] = tile(as_ref[...], ms_ref[...], xs_ref[...])


def _outproj(l, attn, ssm, x, attn_s, ssm_s, xs, ga, gs, gp, w):
    m, ns = x.shape[0], xs.shape[0]
    nw, nt = D_MODEL // TN_W, m // TM_RES
    tile = lambda s: (_tile_index(s, nw, nt), 0)
    const = lambda s: (0, 0)
    gain = lambda n: pl.BlockSpec((None, 1, n), lambda s: (l, 0, 0))
    return pl.pallas_call(
        functools.partial(_outproj_kernel, nw=nw, nt=nt),
        grid=(nw + nt,),
        in_specs=[
            pl.BlockSpec((TM_RES, Q_WIDTH), tile),
            pl.BlockSpec((TM_RES, SSM_WIDTH), tile),
            pl.BlockSpec((TM_RES, D_MODEL), tile),
            pl.BlockSpec((ns, Q_WIDTH), const),
            pl.BlockSpec((ns, SSM_WIDTH), const),
            pl.BlockSpec((ns, D_MODEL), const),
            gain(Q_WIDTH), gain(SSM_WIDTH), gain(D_MODEL),
            pl.BlockSpec((None, D_MODEL, TN_W), lambda s: (l, 0, jnp.minimum(s, nw - 1))),
        ],
        out_specs=[pl.BlockSpec((TM_RES, D_MODEL), tile), pl.BlockSpec((ns, D_MODEL), const)],
        out_shape=[jax.ShapeDtypeStruct((m, D_MODEL), F32), jax.ShapeDtypeStruct((ns, D_MODEL), F32)],
        scratch_shapes=[pltpu.VMEM((D_MODEL, D_MODEL), BF16), pltpu.VMEM((TM_RES, D_MODEL), BF16),
                        pltpu.VMEM((TM_RES, D_MODEL), F32)],
        compiler_params=_params(("arbitrary",)),
        name="outproj",
    )(attn, ssm, x, attn_s, ssm_s, xs, ga, gs, gp, w)


def _ffn_kernel(x_ref, xs_ref, g1_ref, wg_ref, wu_ref, wd_ref, o_ref, os_ref, h_ref):
    i = pl.program_id(0)
    j = pl.program_id(1)
    tm = x_ref.shape[0]

    def swiglu_down(h):
        gate = jnp.dot(h, wg_ref[...].astype(BF16), preferred_element_type=F32)
        up = jnp.dot(h, wu_ref[...].astype(BF16), preferred_element_type=F32)
        act = (gate * jax.nn.sigmoid(gate) * up).astype(BF16)
        return jnp.dot(act, wd_ref[...].astype(BF16), preferred_element_type=F32)

    @pl.when((i == 0) & (j == 0))
    def _():
        h = jnp.concatenate([_rms(x_ref[...], g1_ref[...]).astype(BF16),
                             _rms(xs_ref[...], g1_ref[...]).astype(BF16)], axis=0)
        h_ref[...] = h
        down = swiglu_down(h)
        o_ref[...] = down[:tm]
        os_ref[...] = down[tm:]

    @pl.when((i == 0) & (j > 0))
    def _():
        down = swiglu_down(h_ref[...])
        o_ref[...] += down[:tm]
        os_ref[...] += down[tm:]

    @pl.when((i > 0) & (j == 0))
    def _():
        h = _rms(x_ref[...], g1_ref[...]).astype(BF16)
        h_ref[:tm, :] = h
        o_ref[...] = swiglu_down(h)

    @pl.when((i > 0) & (j > 0))
    def _():
        o_ref[...] += swiglu_down(h_ref[:tm, :])


def _ffn(l, x, xs, g1, wgu, wd):
    m, ns = x.shape[0], xs.shape[0]
    nf = D_FF // TF_FFN
    gain = pl.BlockSpec((None, 1, D_MODEL), lambda i, j: (l, 0, 0))
    return pl.pallas_call(
        _ffn_kernel,
        grid=(m // TM_FFN, nf),
        in_specs=[
            pl.BlockSpec((TM_FFN, D_MODEL), lambda i, j: (i, 0), pipeline_mode=pl.Buffered(1)),
            pl.BlockSpec((ns, D_MODEL), lambda i, j: (0, 0)),
            gain,
            pl.BlockSpec((None, D_MODEL, TF_FFN), lambda i, j: (l, 0, j)),
            pl.BlockSpec((None, D_MODEL, TF_FFN), lambda i, j: (l, 0, nf + j)),
            pl.BlockSpec((None, TF_FFN, D_MODEL), lambda i, j: (l, j, 0)),
        ],
        out_specs=[pl.BlockSpec((TM_FFN, D_MODEL), lambda i, j: (i, 0)),
                   pl.BlockSpec((ns, D_MODEL), lambda i, j: (0, 0))],
        out_shape=[jax.ShapeDtypeStruct((m, D_MODEL), F32), jax.ShapeDtypeStruct((ns, D_MODEL), F32)],
        scratch_shapes=[pltpu.VMEM((TM_FFN + ns, D_MODEL), BF16)],
        compiler_params=_params(("arbitrary", "arbitrary")),
        name="ffn",
    )(x, xs, g1, wgu, wgu, wd)


def _ple_kernel(x_ref, f_ref, pe_ref, xs_ref, fs_ref, pes_ref, g_ref, wg_ref, wp_ref, o_ref, os_ref,
                wgres_ref, wpres_ref, x0_ref, pe0_ref, *, nw, nt):
    s = pl.program_id(0)

    def ffn_residual(x1, f):
        return x1 + _rms(f, g_ref[...])

    def tile(x1, f, pe):
        x2 = ffn_residual(x1, f)
        gate = jnp.dot(x2.astype(BF16), wgres_ref[...], preferred_element_type=F32)
        proj = jnp.dot(pe.astype(BF16), wpres_ref[...], preferred_element_type=F32)
        return x2 + jax.nn.sigmoid(gate) * proj

    @pl.when(s == 0)
    def _():
        x2 = ffn_residual(x_ref[...], f_ref[...])
        o_ref[...] = x2
        x0_ref[...] = x2.astype(BF16)
        pe0_ref[...] = pe_ref[...].astype(BF16)

    def tile0(cols, gate_chunk):
        proj_chunk = wp_ref[...].astype(BF16)
        wpres_ref[:, cols] = proj_chunk
        gate = jnp.dot(x0_ref[...], gate_chunk, preferred_element_type=F32)
        proj = jnp.dot(pe0_ref[...], proj_chunk, preferred_element_type=F32)
        o_ref[:, cols] = o_ref[:, cols] + jax.nn.sigmoid(gate) * proj

    _on_weight_chunks(s, wg_ref, wgres_ref, nw, tile0)

    @pl.when(_is_tile_step(s, nw, nt))
    def _():
        o_ref[...] = tile(x_ref[...], f_ref[...], pe_ref[...])

    @pl.when(s == nw + nt - 1)
    def _():
        os_ref[...] = tile(xs_ref[...], fs_ref[...], pes_ref[...])


def _ple(l, x, f, pe, xs, fs, pes, g, wg, wp):
    m, ns = x.shape[0], xs.shape[0]
    nw, nt = D_MODEL // TN_W, m // TM_RES
    tile = lambda s: (_tile_index(s, nw, nt), 0)
    const = lambda s: (0, 0)
    chunk = lambda s: (l, 0, jnp.minimum(s, nw - 1))
    return pl.pallas_call(
        functools.partial(_ple_kernel, nw=nw, nt=nt),
        grid=(nw + nt,),
        in_specs=[
            pl.BlockSpec((TM_RES, D_MODEL), tile),
            pl.BlockSpec((TM_RES, D_MODEL), tile),
            pl.BlockSpec((None, TM_RES, PLE_DIM), lambda s: (l, _tile_index(s, nw, nt), 0)),
            pl.BlockSpec((ns, D_MODEL), const),
            pl.BlockSpec((ns, D_MODEL), const),
            pl.BlockSpec((None, ns, PLE_DIM), lambda s: (l, 0, 0)),
            pl.BlockSpec((None, 1, D_MODEL), lambda s: (l, 0, 0)),
            pl.BlockSpec((None, D_MODEL, TN_W), chunk),
            pl.BlockSpec((None, PLE_DIM, TN_W), chunk),
        ],
        out_specs=[pl.BlockSpec((TM_RES, D_MODEL), tile), pl.BlockSpec((ns, D_MODEL), const)],
        out_shape=[jax.ShapeDtypeStruct((m, D_MODEL), F32), jax.ShapeDtypeStruct((ns, D_MODEL), F32)],
        scratch_shapes=[pltpu.VMEM((D_MODEL, D_MODEL), BF16), pltpu.VMEM((PLE_DIM, D_MODEL), BF16),
                        pltpu.VMEM((TM_RES, D_MODEL), BF16), pltpu.VMEM((TM_RES, PLE_DIM), BF16)],
        compiler_params=_params(("arbitrary",), vmem_limit=VMEM_LIMIT_FUSED),
        name="ple",
    )(x, f, pe, xs, fs, pes, g, wg, wp)


def kernel(x_prompt, x_sample, cache_k, cache_v, state_ssm_re, state_ssm_im, p_prompt, p_sample,
           g_pre_mix, w_in, attn_sinks, ssm_a_re, ssm_a_im, ssm_log_dt, ssm_b_re, ssm_b_im,
           ssm_c_re, ssm_c_im, ssm_d, ssm_w_glu, g_attn_out, g_ssm_out, w_out, g_post_mix,
           g_pre_ffn, w_gate_up, w_down, g_post_ffn, w_ple_gate, w_ple_proj):
    nb, t, _ = x_prompt.shape
    ns = x_sample.shape[0]

    wb, wbd, wcr, wci, wglu, a_dense, a_row = _ssm_prep(ssm_a_re, ssm_a_im, ssm_log_dt, ssm_b_re, ssm_b_im,
                                                        ssm_c_re, ssm_c_im, ssm_w_glu)
    d_row = ssm_d.reshape(DEPTH, N_SLABS, 1, SLAB_CH)
    slope_col = jnp.asarray(_SLOPES, F32).reshape(N_HEADS, 1)
    sink_col = attn_sinks.reshape(DEPTH, N_HEADS, 1)
    gain = lambda g: g.reshape(DEPTH, 1, g.shape[-1])
    g_pre_mix, g_attn_out, g_ssm_out, g_post_mix, g_pre_ffn, g_post_ffn = map(
        gain, (g_pre_mix, g_attn_out, g_ssm_out, g_post_mix, g_pre_ffn, g_post_ffn))

    xp = x_prompt.reshape(nb * t, D_MODEL)
    xs = x_sample.reshape(ns, D_MODEL)
    pp = p_prompt.reshape(DEPTH, nb * t, PLE_DIM)
    ps = p_sample.reshape(DEPTH, ns, PLE_DIM)
    s0r = state_ssm_re.reshape(DEPTH, ns, SSM_GROUPS * SSM_STATE)
    s0i = state_ssm_im.reshape(DEPTH, ns, SSM_GROUPS * SSM_STATE)
    ck = cache_k.reshape(DEPTH, ns, WINDOW, KV_WIDTH)
    cv = cache_v.reshape(DEPTH, ns, WINDOW, KV_WIDTH)

    kv_lo, kv_hi = Q_WIDTH, Q_WIDTH + 2 * KV_WIDTH

    kvp_l, srp_l, sip_l, ks_l, vs_l, srs_l, sis_l = [], [], [], [], [], [], []
    for l in range(DEPTH):
        z, zs, attn = _inproj_attn(l, xp, xs, g_pre_mix, w_in, attn_sinks, t)
        z3 = z.reshape(nb, t, IN_WIDTH)

        ssm, sre, sim = _ssm_prompt(l, z3, wbd, a_dense, wcr, wci, d_row, wglu)

        kvn = zs[:, kv_lo:kv_hi].reshape(ns, 1, 2 * KV_WIDTH)
        attn_s, nks, nvs = _attn_sample(l, zs[:, :Q_WIDTH].reshape(ns, N_HEADS, HEAD_DIM), kvn,
                                        ck, cv, sink_col, slope_col)
        ssm_s, srs, sis = _ssm_sample(l, zs, wb, a_row, s0r, s0i, wcr, wci, d_row, wglu)

        x1, x1s = _outproj(l, attn, ssm.reshape(nb * t, SSM_WIDTH), xp,
                           attn_s.reshape(ns, Q_WIDTH), ssm_s, xs,
                           g_attn_out, g_ssm_out, g_post_mix, w_out)
        f, fs = _ffn(l, x1, x1s, g_pre_ffn, w_gate_up, w_down)
        xp, xs = _ple(l, x1, f, pp, x1s, fs, ps, g_post_ffn, w_ple_gate, w_ple_proj)

        kvp_l.append(z3[:, t - WINDOW:, kv_lo:kv_hi])
        ks_l.append(nks)
        vs_l.append(nvs)
        srp_l.append(sre)
        sip_l.append(sim)
        srs_l.append(srs)
        sis_l.append(sis)

    heads = lambda a: a.reshape(a.shape[:-1] + (N_KV, HEAD_DIM))
    kvp = jnp.stack(kvp_l)
    new_ks = jnp.stack(ks_l)
    new_vs = jnp.stack(vs_l)
    unslab = lambda s: jnp.swapaxes(jnp.stack(s), 1, 2).reshape(DEPTH, nb, SSM_GROUPS, SSM_STATE)
    states = lambda s: jnp.stack(s).reshape(DEPTH, ns, SSM_GROUPS, SSM_STATE)
    return (xp.reshape(nb, t, D_MODEL), xs.reshape(ns, 1, D_MODEL),
            heads(kvp[..., :KV_WIDTH]), heads(kvp[..., KV_WIDTH:]), unslab(srp_l), unslab(sip_l),
            heads(new_ks), heads(new_vs), states(srs_l), states(sis_l))
```

```python
import functools
import math

import numpy as np
import jax
import jax.numpy as jnp
from jax import lax
from jax.experimental import pallas as pl
from jax.experimental.pallas import tpu as pltpu

F32 = jnp.float32
BF16 = jnp.bfloat16

D_MODEL = 2048
DEPTH = 4
HEAD_DIM = 64
N_HEADS = 16
N_KV = 2
GRP = N_HEADS // N_KV
WINDOW = 128
Q_WIDTH = N_HEADS * HEAD_DIM
KV_WIDTH = N_KV * HEAD_DIM
SSM_WIDTH = 1024
SSM_CG = 16
SSM_GROUPS = 64
SSM_STATE = 64
IN_WIDTH = Q_WIDTH + 2 * KV_WIDTH + SSM_WIDTH
D_FF = 5632
PLE_DIM = 256
EPS = 1e-6
NEG_INF = -1e30

LANES = 128
SUBLANES = 8

SLAB_GROUPS = 16
N_SLABS = SSM_GROUPS // SLAB_GROUPS
SLAB_CH = SLAB_GROUPS * SSM_CG
SLAB_ST = SLAB_GROUPS * SSM_STATE
ST_TILES = SLAB_ST // LANES
SSM_CHUNK = 256
SLABS_PER_STEP = 2

TM_RES = 512
TN_W = 256
TM_FFN = 1024
TF_FFN = 256
SAMPLE_PER_STEP = 8

VMEM_LIMIT = 56 * 1024 * 1024
VMEM_LIMIT_FUSED = 60 * 1024 * 1024

_SLOPES = [float(v) for v in
           (2.0 ** (-8.0 * np.arange(1, N_HEADS + 1, dtype=np.float32) / N_HEADS)).astype(np.float32)]


def _rms(x, g):
    var = jnp.mean(x * x, axis=-1, keepdims=True)
    return x * lax.rsqrt(var + EPS) * g


def _iota2(shape):
    return (lax.broadcasted_iota(jnp.int32, shape, 0), lax.broadcasted_iota(jnp.int32, shape, 1))


def _gelu_tanh(x):
    c = math.sqrt(2.0 / math.pi)
    return 0.5 * x * (1.0 + jnp.tanh(c * (x + 0.044715 * (x * x * x))))


def _params(sem, vmem_limit=VMEM_LIMIT):
    return pltpu.CompilerParams(dimension_semantics=sem, vmem_limit_bytes=vmem_limit)


def _tile_index(s, nw, nt):
    return jnp.minimum(jnp.maximum(s - nw + 1, 0), nt - 1)


def _is_tile_step(s, nw, nt):
    return (s >= nw) & (s < nw + nt - 1)


def _on_weight_chunks(s, w_ref, wres_ref, nw, apply):
    tn = w_ref.shape[1]
    for c in range(nw):
        @pl.when(s == c)
        def _(c=c):
            chunk = w_ref[...].astype(BF16)
            wres_ref[:, c * tn:(c + 1) * tn] = chunk
            apply(slice(c * tn, (c + 1) * tn), chunk)


def _attend_tile(tau, l, tiles_per_seq, sink_ref, bias_ref, zq_ref, zkv_ref, o_ref):
    tau = jnp.asarray(tau, jnp.int32)
    qslot = lax.rem(tau, 2)
    cur = lax.rem(tau, 3)
    prev = lax.rem(tau + 2, 3)
    seq_start = lax.rem(tau, tiles_per_seq) == 0
    lane = lax.broadcasted_iota(jnp.int32, (1, 2 * WINDOW), 1)
    hide_prev = jnp.where(seq_start & (lane < WINDOW), NEG_INF, 0.0)
    for bq in range(TM_RES // WINDOW):
        rows = slice(WINDOW * bq, WINDOW * (bq + 1))
        if bq == 0:
            kv_prev = zkv_ref[prev, TM_RES - WINDOW:TM_RES, :]
        else:
            kv_prev = zkv_ref[cur, WINDOW * (bq - 1):WINDOW * bq, :]
        kvcat = jnp.concatenate([kv_prev, zkv_ref[cur, rows, :]], axis=0)
        q = zq_ref[qslot, rows, :]
        for kv in range(N_KV):
            heads = range(GRP * kv, GRP * (kv + 1))
            kh = kvcat[:, HEAD_DIM * kv:HEAD_DIM * (kv + 1)]
            vh = kvcat[:, KV_WIDTH + HEAD_DIM * kv:KV_WIDTH + HEAD_DIM * (kv + 1)]
            qg = jnp.concatenate([q[:, HEAD_DIM * h:HEAD_DIM * (h + 1)] for h in heads], axis=0)
            s_all = lax.dot_general(qg, kh, (((1,), (1,)), ((), ())), preferred_element_type=F32)
            e_all, denoms = [], []
            for i, h in enumerate(heads):
                sink = sink_ref[l, h]
                s = s_all[WINDOW * i:WINDOW * (i + 1)] + bias_ref[h]
                if bq == 0:
                    s = s + hide_prev
                m = jnp.maximum(jnp.max(s, axis=-1, keepdims=True), sink)
                e = jnp.exp(s - m)
                denoms.append(jnp.sum(e, axis=-1, keepdims=True) + jnp.exp(sink - m))
                e_all.append(e.astype(BF16))
            o_all = jnp.dot(jnp.concatenate(e_all, axis=0), vh, preferred_element_type=F32)
            for i, h in enumerate(heads):
                o_ref[rows, HEAD_DIM * h:HEAD_DIM * (h + 1)] = o_all[WINDOW * i:WINDOW * (i + 1)] / denoms[i]


def _inproj_attn_kernel(sink_ref, x_ref, xs_ref, g_ref, w_ref, o_ref, os_ref, oa_ref,
                        wres_ref, h0_ref, zq_ref, zkv_ref, bias_ref, *, l, nw, nt, tiles_per_seq):
    s = pl.program_id(0)
    kv_cols = slice(Q_WIDTH, Q_WIDTH + 2 * KV_WIDTH)

    @pl.when(s == 0)
    def _():
        h0_ref[...] = _rms(x_ref[...], g_ref[...]).astype(BF16)
        zkv_ref[...] = jnp.zeros_like(zkv_ref)
        qi, kj = _iota2((WINDOW, 2 * WINDOW))
        dist_i = qi - kj + WINDOW
        band = (dist_i >= 0) & (dist_i <= WINDOW)
        dist = dist_i.astype(F32)
        for h in range(N_HEADS):
            bias_ref[h] = jnp.where(band, -(_SLOPES[h] * dist), NEG_INF)

    def tile0(cols, chunk):
        o_ref[:, cols] = jnp.dot(h0_ref[...], chunk, preferred_element_type=F32)

    _on_weight_chunks(s, w_ref, wres_ref, nw, tile0)

    def keep_for_attention(t):
        zq_ref[lax.rem(t, 2)] = (o_ref[:, :Q_WIDTH] * (HEAD_DIM ** -0.5)).astype(BF16)
        zkv_ref[lax.rem(t, 3)] = o_ref[:, kv_cols].astype(BF16)

    @pl.when(s == nw - 1)
    def _():
        keep_for_attention(jnp.int32(0))

    @pl.when(_is_tile_step(s, nw, nt))
    def _():
        t = s - nw + 1
        _attend_tile(t - 1, l, tiles_per_seq, sink_ref, bias_ref, zq_ref, zkv_ref, oa_ref)
        h = _rms(x_ref[...], g_ref[...]).astype(BF16)
        o_ref[...] = jnp.dot(h, wres_ref[...], preferred_element_type=F32)
        keep_for_attention(t)

    @pl.when(s == nw + nt - 1)
    def _():
        _attend_tile(nt - 1, l, tiles_per_seq, sink_ref, bias_ref, zq_ref, zkv_ref, oa_ref)
        h = _rms(xs_ref[...], g_ref[...]).astype(BF16)
        os_ref[...] = jnp.dot(h, wres_ref[...], preferred_element_type=F32)


def _inproj_attn(l, x, xs, g, w, sinks, t):
    m, ns = x.shape[0], xs.shape[0]
    n = w.shape[2]
    nw, nt = n // TN_W, m // TM_RES
    tile = lambda s: (_tile_index(s, nw, nt), 0)
    attn_tile = lambda s: (jnp.minimum(jnp.maximum(s - nw, 0), nt - 1), 0)
    const = lambda s: (0, 0)
    return pl.pallas_call(
        functools.partial(_inproj_attn_kernel, l=l, nw=nw, nt=nt, tiles_per_seq=t // TM_RES),
        grid=(nw + nt,),
        in_specs=[
            pl.BlockSpec(memory_space=pltpu.SMEM),
            pl.BlockSpec((TM_RES, D_MODEL), tile),
            pl.BlockSpec((ns, D_MODEL), const),
            pl.BlockSpec((None, 1, D_MODEL), lambda s: (l, 0, 0)),
            pl.BlockSpec((None, D_MODEL, TN_W), lambda s: (l, 0, jnp.minimum(s, nw - 1))),
        ],
        out_specs=[pl.BlockSpec((TM_RES, n), tile), pl.BlockSpec((ns, n), const),
                   pl.BlockSpec((TM_RES, Q_WIDTH), attn_tile)],
        out_shape=[jax.ShapeDtypeStruct((m, n), F32), jax.ShapeDtypeStruct((ns, n), F32),
                   jax.ShapeDtypeStruct((m, Q_WIDTH), F32)],
        scratch_shapes=[pltpu.VMEM((D_MODEL, n), BF16), pltpu.VMEM((TM_RES, D_MODEL), BF16),
                        pltpu.VMEM((2, TM_RES, Q_WIDTH), BF16), pltpu.VMEM((3, TM_RES, 2 * KV_WIDTH), BF16),
                        pltpu.VMEM((N_HEADS, WINDOW, 2 * WINDOW), F32)],
        compiler_params=_params(("arbitrary",), vmem_limit=VMEM_LIMIT_FUSED),
        name="inproj_attn",
    )(sinks, x, xs, g, w)


def _attn_sample_kernel(q_ref, kvn_ref, ck_ref, cv_ref, sink_ref, slope_ref, o_ref, nk_ref, nv_ref):
    sp = q_ref.shape[0]
    rows_n = sp * GRP
    dist = (WINDOW - lax.broadcasted_iota(jnp.int32, (rows_n, WINDOW), 1)).astype(F32)
    r, c = _iota2((rows_n, sp * HEAD_DIM))
    own = (r >> (GRP.bit_length() - 1)) == (c >> (HEAD_DIM.bit_length() - 1))
    per_row = lambda col: jnp.concatenate([col] * sp, axis=0)
    for kv in range(N_KV):
        lo, hi = HEAD_DIM * kv, HEAD_DIM * (kv + 1)
        slope = per_row(slope_ref[GRP * kv:GRP * (kv + 1), :])
        sink = per_row(sink_ref[GRP * kv:GRP * (kv + 1), :])
        qk = q_ref[:, GRP * kv:GRP * (kv + 1), :].reshape(rows_n, HEAD_DIM)
        q_bd = jnp.where(own, jnp.concatenate([qk] * sp, axis=1), 0.0).astype(BF16)
        k_all = jnp.concatenate([ck_ref[n, :, lo:hi] for n in range(sp)], axis=1).astype(BF16)
        v_all = jnp.concatenate([cv_ref[n, :, lo:hi] for n in range(sp)], axis=1).astype(BF16)
        s_c = lax.dot_general(q_bd, k_all, (((1,), (1,)), ((), ())),
                              preferred_element_type=F32) * (HEAD_DIM ** -0.5)
        s_c = s_c - slope * dist
        kn = kvn_ref[:, :, lo:hi]
        vn = kvn_ref[:, :, KV_WIDTH + lo:KV_WIDTH + hi]
        kn_rows = jnp.broadcast_to(kn, (sp, GRP, HEAD_DIM)).reshape(rows_n, HEAD_DIM)
        s_n = jnp.sum(qk * kn_rows, axis=-1, keepdims=True) * (HEAD_DIM ** -0.5)
        m = jnp.maximum(jnp.maximum(jnp.max(s_c, axis=-1, keepdims=True), s_n), sink)
        e_c = jnp.exp(s_c - m)
        e_n = jnp.exp(s_n - m)
        denom = jnp.sum(e_c, axis=-1, keepdims=True) + e_n + jnp.exp(sink - m)
        o_all = jnp.dot(e_c.astype(BF16), v_all, preferred_element_type=F32)
        for n in range(sp):
            rows = slice(GRP * n, GRP * (n + 1))
            o = o_all[rows, HEAD_DIM * n:HEAD_DIM * (n + 1)] + e_n[rows] * vn[n]
            o_ref[n, GRP * kv:GRP * (kv + 1), :] = o / denom[rows]

    for n in range(sp):
        nk_ref[n, 0:WINDOW - 1, :] = ck_ref[n, 1:WINDOW, :]
        nk_ref[n, WINDOW - 1:WINDOW, :] = kvn_ref[n, :, :KV_WIDTH]
        nv_ref[n, 0:WINDOW - 1, :] = cv_ref[n, 1:WINDOW, :]
        nv_ref[n, WINDOW - 1:WINDOW, :] = kvn_ref[n, :, KV_WIDTH:]


def _attn_sample(l, q3, kvn, ck, cv, sink_col, slope_col):
    n = q3.shape[0]
    sp = SAMPLE_PER_STEP
    return pl.pallas_call(
        _attn_sample_kernel,
        grid=(n // sp,),
        in_specs=[
            pl.BlockSpec((sp, N_HEADS, HEAD_DIM), lambda i: (i, 0, 0)),
            pl.BlockSpec((sp, 1, 2 * KV_WIDTH), lambda i: (i, 0, 0)),
            pl.BlockSpec((None, sp, WINDOW, KV_WIDTH), lambda i: (l, i, 0, 0)),
            pl.BlockSpec((None, sp, WINDOW, KV_WIDTH), lambda i: (l, i, 0, 0)),
            pl.BlockSpec((None, N_HEADS, 1), lambda i: (l, 0, 0)),
            pl.BlockSpec((N_HEADS, 1), lambda i: (0, 0)),
        ],
        out_specs=[pl.BlockSpec((sp, N_HEADS, HEAD_DIM), lambda i: (i, 0, 0)),
                   pl.BlockSpec((sp, WINDOW, KV_WIDTH), lambda i: (i, 0, 0)),
                   pl.BlockSpec((sp, WINDOW, KV_WIDTH), lambda i: (i, 0, 0))],
        out_shape=[jax.ShapeDtypeStruct((n, N_HEADS, HEAD_DIM), F32),
                   jax.ShapeDtypeStruct((n, WINDOW, KV_WIDTH), F32),
                   jax.ShapeDtypeStruct((n, WINDOW, KV_WIDTH), F32)],
        compiler_params=_params(("arbitrary",)),
        name="attn_sample",
    )(q3, kvn, ck, cv, sink_col, slope_col)


def _discretise(a_re, a_im, log_dt):
    dt = jnp.exp(log_dt)
    dta_re = dt * a_re
    dta_im = dt * a_im
    mag = jnp.exp(dta_re)
    return mag * jnp.cos(dta_im), mag * jnp.sin(dta_im)


def _ssm_prep_kernel(are_ref, aim_ref, ldt_ref, bre_ref, bim_ref, cre_ref, cim_ref, wg_ref,
                     ared_ref, aimd_ref, ldtd_ref,
                     wb_ref, wbd_ref, wcr_ref, wci_ref, wglu_ref, ad_ref, ar_ref):
    copy_dot = lambda a, b: jnp.dot(a.astype(BF16), b, preferred_element_type=F32)

    a_re = are_ref[...]
    a_im = aim_ref[...]
    ab_re, ab_im = _discretise(a_re, a_im, ldt_ref[...])
    ar_ref[0] = ab_re
    ar_ref[1] = ab_im
    den = a_re * a_re + a_im * a_im
    f_re = ((ab_re - 1.0) * a_re + ab_im * a_im) / den
    f_im = (ab_im * a_re - (ab_re - 1.0) * a_im) / den
    r, _ = _iota2((LANES, SLAB_ST))
    f_t = jnp.where(r == 0, f_re, jnp.where(r == 1, f_im, 0.0)).T
    f_re = f_t[:, 0:1]
    f_im = f_t[:, 1:2]

    b_re = bre_ref[...].reshape(SLAB_ST, SSM_CG)
    b_im = bim_ref[...].reshape(SLAB_ST, SSM_CG)
    c_re = cre_ref[...].reshape(SLAB_CH, SSM_STATE)
    c_im = cim_ref[...].reshape(SLAB_CH, SSM_STATE)
    w_glu = wg_ref[...].reshape(SLAB_CH, 2 * SSM_CG)
    r, c = _iota2((SSM_CG, SLAB_CH))
    tile_c = (r == (c & (SSM_CG - 1))).astype(BF16)
    r, c = _iota2((SLAB_ST, SLAB_CH))
    diag = (r >> 6) == (c >> 4)
    bbt_re = jnp.where(diag, copy_dot(f_re * b_re - f_im * b_im, tile_c), 0.0)
    bbt_im = jnp.where(diag, copy_dot(f_re * b_im + f_im * b_re, tile_c), 0.0)
    wb_re = bbt_re.T
    wb_im = bbt_im.T
    wb_ref[:, :SLAB_ST] = wb_re.astype(BF16)
    wb_ref[:, SLAB_ST:] = wb_im.astype(BF16)
    r, _ = _iota2((SLAB_CH, LANES))
    own = [(r >> 5) == k for k in range(ST_TILES)]
    fold = lambda w: sum(jnp.where(own[k], w[:, LANES * k:LANES * (k + 1)], 0.0) for k in range(ST_TILES))
    wbd_ref[:, :LANES] = fold(wb_re).astype(BF16)
    wbd_ref[:, LANES:] = fold(wb_im).astype(BF16)

    r, c = _iota2((SSM_STATE, SLAB_ST))
    tile_p = (r == (c & (SSM_STATE - 1))).astype(BF16)
    r, c = _iota2((SLAB_CH, SLAB_ST))
    diag = (r >> 4) == (c >> 6)
    wcr_ref[...] = jnp.where(diag, copy_dot(c_re, tile_p), 0.0).T.astype(BF16)
    wci_ref[...] = jnp.where(diag, copy_dot(c_im, tile_p), 0.0).T.astype(BF16)

    r, c = _iota2((2 * SSM_CG, 2 * SLAB_CH))
    tile_e = (r == (c & (SSM_CG - 1)) + SSM_CG * (c >> 8)).astype(BF16)
    r, c = _iota2((SLAB_CH, 2 * SLAB_CH))
    diag = (r >> 4) == ((c & (SLAB_CH - 1)) >> 4)
    wglu_ref[...] = jnp.where(diag, copy_dot(w_glu, tile_e), 0.0).astype(BF16)

    abd_re, abd_im = _discretise(ared_ref[...], aimd_ref[...], ldtd_ref[...])
    ad_ref[0] = abd_re
    ad_ref[1] = abd_im


def _ssm_prep(a_re, a_im, log_dt, b_re, b_im, c_re, c_im, w_glu):
    ldt_gp = jnp.repeat(log_dt, SSM_STATE, axis=1)
    row = lambda a: a.reshape(DEPTH, N_SLABS, 1, SLAB_ST)
    dense = lambda a: a.reshape(DEPTH, N_SLABS, SUBLANES, LANES)
    blk = lambda rows, cols: pl.BlockSpec((None, SLAB_GROUPS, rows, cols), lambda l, s: (l, s, 0, 0))
    rblk = pl.BlockSpec((None, None, 1, SLAB_ST), lambda l, s: (l, s, 0, 0))
    dblk = pl.BlockSpec((None, None, SUBLANES, LANES), lambda l, s: (l, s, 0, 0))
    oblk = lambda rows, cols: pl.BlockSpec((None, None, rows, cols), lambda l, s: (l, s, 0, 0))
    return pl.pallas_call(
        _ssm_prep_kernel,
        grid=(DEPTH, N_SLABS),
        in_specs=[rblk, rblk, rblk,
                  blk(SSM_STATE, SSM_CG), blk(SSM_STATE, SSM_CG),
                  blk(SSM_CG, SSM_STATE), blk(SSM_CG, SSM_STATE), blk(SSM_CG, 2 * SSM_CG),
                  dblk, dblk, dblk],
        out_specs=[oblk(SLAB_CH, 2 * SLAB_ST), oblk(SLAB_CH, 2 * LANES),
                   oblk(SLAB_ST, SLAB_CH), oblk(SLAB_ST, SLAB_CH), oblk(SLAB_CH, 2 * SLAB_CH),
                   pl.BlockSpec((None, None, 2, SUBLANES, LANES), lambda l, s: (l, s, 0, 0, 0)),
                   pl.BlockSpec((None, None, 2, 1, SLAB_ST), lambda l, s: (l, s, 0, 0, 0))],
        out_shape=[jax.ShapeDtypeStruct((DEPTH, N_SLABS, SLAB_CH, 2 * SLAB_ST), BF16),
                   jax.ShapeDtypeStruct((DEPTH, N_SLABS, SLAB_CH, 2 * LANES), BF16),
                   jax.ShapeDtypeStruct((DEPTH, N_SLABS, SLAB_ST, SLAB_CH), BF16),
                   jax.ShapeDtypeStruct((DEPTH, N_SLABS, SLAB_ST, SLAB_CH), BF16),
                   jax.ShapeDtypeStruct((DEPTH, N_SLABS, SLAB_CH, 2 * SLAB_CH), BF16),
                   jax.ShapeDtypeStruct((DEPTH, N_SLABS, 2, SUBLANES, LANES), F32),
                   jax.ShapeDtypeStruct((DEPTH, N_SLABS, 2, 1, SLAB_ST), F32)],
        compiler_params=_params(("arbitrary", "arbitrary")),
        name="ssm_prep",
    )(row(a_re), row(a_im), row(ldt_gp), b_re, b_im, c_re, c_im, w_glu,
      dense(a_re), dense(a_im), dense(ldt_gp))


def _ssm_tail(s_re, s_im, u, wcr, wci, d, wglu):
    y = (jnp.dot(s_re.astype(BF16), wcr, preferred_element_type=F32)
         - jnp.dot(s_im.astype(BF16), wci, preferred_element_type=F32)
         + d * u)
    z = jnp.dot(_gelu_tanh(y).astype(BF16), wglu, preferred_element_type=F32)
    return z[:, :SLAB_CH] * jax.nn.sigmoid(z[:, SLAB_CH:])


def _ssm_prompt_kernel(*refs):
    u_refs = refs[:SLABS_PER_STEP]
    (wbd_ref, a_ref, wcr_ref, wci_ref, d_ref, wglu_ref,
     o_ref, sre_ref, sim_ref, st_ref, carry_ref) = refs[SLABS_PER_STEP:]
    nb = u_refs[0].shape[0]
    chains = [(p, b) for p in range(SLABS_PER_STEP) for b in range(nb)]

    @pl.when(pl.program_id(1) == 0)
    def _():
        carry_ref[...] = jnp.zeros_like(carry_ref)

    sub, ch = _iota2((SUBLANES, SLAB_CH))
    own = sub == (ch >> 5)
    for p, b in chains:
        lhs = jnp.where(own[None], u_refs[p][b][:, None, :], 0.0).astype(BF16)
        bu = jnp.dot(lhs.reshape(SUBLANES * SSM_CHUNK, SLAB_CH), wbd_ref[p],
                     preferred_element_type=F32)
        st_ref[p, b, 0] = bu[:, :LANES]
        st_ref[p, b, 1] = bu[:, LANES:]

    abar = [(a_ref[p, 0], a_ref[p, 1]) for p in range(SLABS_PER_STEP)]

    def step(t, carry):
        row = pl.multiple_of(t * SUBLANES, SUBLANES)
        new = []
        for i, (p, b) in enumerate(chains):
            a_re, a_im = abar[p]
            sr, si = carry[2 * i], carry[2 * i + 1]
            nr = a_re * sr - a_im * si + st_ref[p, b, 0, pl.ds(row, SUBLANES), :]
            ni = a_re * si + a_im * sr + st_ref[p, b, 1, pl.ds(row, SUBLANES), :]
            st_ref[p, b, 0, pl.ds(row, SUBLANES), :] = nr
            st_ref[p, b, 1, pl.ds(row, SUBLANES), :] = ni
            new += [nr, ni]
        return tuple(new)

    init = tuple(carry_ref[p, b, j] for p, b in chains for j in range(2))
    fin = lax.fori_loop(0, SSM_CHUNK, step, init, unroll=4)
    for i, (p, b) in enumerate(chains):
        carry_ref[p, b, 0] = fin[2 * i]
        carry_ref[p, b, 1] = fin[2 * i + 1]
        sre_ref[p, b] = fin[2 * i]
        sim_ref[p, b] = fin[2 * i + 1]

    tiles = lambda p, b, j: jnp.concatenate(
        [st_ref[p, b, j, pl.ds(k, SSM_CHUNK, stride=SUBLANES), :] for k in range(ST_TILES)], axis=1)
    for p, b in chains:
        o_ref[b, :, SLAB_CH * p:SLAB_CH * (p + 1)] = _ssm_tail(
            tiles(p, b, 0), tiles(p, b, 1), u_refs[p][b], wcr_ref[p], wci_ref[p], d_ref[p], wglu_ref[p])


def _slab_specs(l):
    w4 = lambda rows, cols: pl.BlockSpec((None, None, rows, cols), lambda s, *_: (l, s, 0, 0))
    return dict(wb=w4(SLAB_CH, 2 * SLAB_ST), wc=w4(SLAB_ST, SLAB_CH),
                d=w4(1, SLAB_CH), wglu=w4(SLAB_CH, 2 * SLAB_CH))


def _ssm_prompt(l, z3, wbd, a_dense, wcr, wci, d_row, wglu):
    nb, t, _ = z3.shape
    sps = SLABS_PER_STEP
    ucol = (Q_WIDTH + 2 * KV_WIDTH) // SLAB_CH
    u_spec = lambda p: pl.BlockSpec((nb, SSM_CHUNK, SLAB_CH), lambda s, c: (0, c, ucol + sps * s + p))
    w4 = lambda rows, cols: pl.BlockSpec((None, sps, rows, cols), lambda s, c: (l, s, 0, 0))
    st_blk = pl.BlockSpec((sps, nb, SUBLANES, LANES), lambda s, c: (s, 0, 0, 0))
    return pl.pallas_call(
        _ssm_prompt_kernel,
        grid=(N_SLABS // sps, t // SSM_CHUNK),
        in_specs=[u_spec(p) for p in range(sps)] + [
            w4(SLAB_CH, 2 * LANES),
            pl.BlockSpec((None, sps, 2, SUBLANES, LANES), lambda s, c: (l, s, 0, 0, 0)),
            w4(SLAB_ST, SLAB_CH), w4(SLAB_ST, SLAB_CH), w4(1, SLAB_CH), w4(SLAB_CH, 2 * SLAB_CH),
        ],
        out_specs=[pl.BlockSpec((nb, SSM_CHUNK, sps * SLAB_CH), lambda s, c: (0, c, s)), st_blk, st_blk],
        out_shape=[
            jax.ShapeDtypeStruct((nb, t, SSM_WIDTH), F32),
            jax.ShapeDtypeStruct((N_SLABS, nb, SUBLANES, LANES), F32),
            jax.ShapeDtypeStruct((N_SLABS, nb, SUBLANES, LANES), F32),
        ],
        scratch_shapes=[
            pltpu.VMEM((sps, nb, 2, SUBLANES * SSM_CHUNK, LANES), F32),
            pltpu.VMEM((sps, nb, 2, SUBLANES, LANES), F32),
        ],
        compiler_params=_params(("arbitrary", "arbitrary")),
        name="ssm_prompt",
    )(*([z3] * sps), wbd, a_dense, wcr, wci, d_row, wglu)


def _ssm_sample_kernel(u_ref, wb_ref, a_ref, s0r_ref, s0i_ref, wcr_ref, wci_ref, d_ref, wglu_ref,
                       o_ref, sre_ref, sim_ref):
    u = u_ref[...]
    bu = jnp.dot(u.astype(BF16), wb_ref[...], preferred_element_type=F32)
    a_re = a_ref[0]
    a_im = a_ref[1]
    s0r = s0r_ref[...]
    s0i = s0i_ref[...]
    nr = bu[:, :SLAB_ST] + a_re * s0r - a_im * s0i
    ni = bu[:, SLAB_ST:] + a_re * s0i + a_im * s0r
    sre_ref[...] = nr
    sim_ref[...] = ni
    o_ref[...] = _ssm_tail(nr, ni, u, wcr_ref[...], wci_ref[...], d_ref[...], wglu_ref[...])


def _ssm_sample(l, z, wb, a_row, s0r, s0i, wcr, wci, d_row, wglu):
    n = z.shape[0]
    ucol = (Q_WIDTH + 2 * KV_WIDTH) // SLAB_CH
    sp = _slab_specs(l)
    st_in = pl.BlockSpec((None, n, SLAB_ST), lambda s: (l, 0, s))
    st_out = pl.BlockSpec((n, SLAB_ST), lambda s: (0, s))
    return pl.pallas_call(
        _ssm_sample_kernel,
        grid=(N_SLABS,),
        in_specs=[
            pl.BlockSpec((n, SLAB_CH), lambda s: (0, ucol + s)),
            sp["wb"],
            pl.BlockSpec((None, None, 2, 1, SLAB_ST), lambda s: (l, s, 0, 0, 0)),
            st_in, st_in, sp["wc"], sp["wc"], sp["d"], sp["wglu"],
        ],
        out_specs=[pl.BlockSpec((n, SLAB_CH), lambda s: (0, s)), st_out, st_out],
        out_shape=[
            jax.ShapeDtypeStruct((n, SSM_WIDTH), F32),
            jax.ShapeDtypeStruct((n, SSM_GROUPS * SSM_STATE), F32),
            jax.ShapeDtypeStruct((n, SSM_GROUPS * SSM_STATE), F32),
        ],
        compiler_params=_params(("arbitrary",)),
        name="ssm_sample",
    )(z, wb, a_row, s0r, s0i, wcr, wci, d_row, wglu)


def _outproj_kernel(a_ref, m_ref, x_ref, as_ref, ms_ref, xs_ref, ga_ref, gs_ref, gp_ref, w_ref,
                    o_ref, os_ref, wres_ref, m0_ref, y0_ref, *, nw, nt):
    s = pl.program_id(0)

    def merge(attn, ssm):
        return jnp.concatenate([_rms(attn, ga_ref[...]).astype(BF16),
                                _rms(ssm, gs_ref[...]).astype(BF16)], axis=1)

    def tile(attn, ssm, x):
        y = jnp.dot(merge(attn, ssm), wres_ref[...], preferred_element_type=F32)
        return x + _rms(y, gp_ref[...])

    @pl.when(s == 0)
    def _():
        m0_ref[...] = merge(a_ref[...], m_ref[...])

    def tile0(cols, chunk):
        y0_ref[:, cols] = jnp.dot(m0_ref[...], chunk, preferred_element_type=F32)

    _on_weight_chunks(s, w_ref, wres_ref, nw, tile0)

    @pl.when(s == nw - 1)
    def _():
        o_ref[...] = x_ref[...] + _rms(y0_ref[...], gp_ref[...])

    @pl.when(_is_tile_step(s, nw, nt))
    def _():
        o_ref[...] = tile(a_ref[...], m_ref[...], x_ref[...])

    @pl.when(s == nw + nt - 1)
    def _():
        os_ref[...] = tile(as_ref[...], ms_ref[...], xs_ref[...])


def _outproj(l, attn, ssm, x, attn_s, ssm_s, xs, ga, gs, gp, w):
    m, ns = x.shape[0], xs.shape[0]
    nw, nt = D_MODEL // TN_W, m // TM_RES
    tile = lambda s: (_tile_index(s, nw, nt), 0)
    const = lambda s: (0, 0)
    gain = lambda n: pl.BlockSpec((None, 1, n), lambda s: (l, 0, 0))
    return pl.pallas_call(
        functools.partial(_outproj_kernel, nw=nw, nt=nt),
        grid=(nw + nt,),
        in_specs=[
            pl.BlockSpec((TM_RES, Q_WIDTH), tile),
            pl.BlockSpec((TM_RES, SSM_WIDTH), tile),
            pl.BlockSpec((TM_RES, D_MODEL), tile),
            pl.BlockSpec((ns, Q_WIDTH), const),
            pl.BlockSpec((ns, SSM_WIDTH), const),
            pl.BlockSpec((ns, D_MODEL), const),
            gain(Q_WIDTH), gain(SSM_WIDTH), gain(D_MODEL),
            pl.BlockSpec((None, D_MODEL, TN_W), lambda s: (l, 0, jnp.minimum(s, nw - 1))),
        ],
        out_specs=[pl.BlockSpec((TM_RES, D_MODEL), tile), pl.BlockSpec((ns, D_MODEL), const)],
        out_shape=[jax.ShapeDtypeStruct((m, D_MODEL), F32), jax.ShapeDtypeStruct((ns, D_MODEL), F32)],
        scratch_shapes=[pltpu.VMEM((D_MODEL, D_MODEL), BF16), pltpu.VMEM((TM_RES, D_MODEL), BF16),
                        pltpu.VMEM((TM_RES, D_MODEL), F32)],
        compiler_params=_params(("arbitrary",)),
        name="outproj",
    )(attn, ssm, x, attn_s, ssm_s, xs, ga, gs, gp, w)


def _ffn_kernel(x_ref, xs_ref, g1_ref, wg_ref, wu_ref, wd_ref, o_ref, os_ref, h_ref):
    i = pl.program_id(0)
    j = pl.program_id(1)
    tm = x_ref.shape[0]

    def swiglu_down(h):
        gate = jnp.dot(h, wg_ref[...].astype(BF16), preferred_element_type=F32)
        up = jnp.dot(h, wu_ref[...].astype(BF16), preferred_element_type=F32)
        act = (gate * jax.nn.sigmoid(gate) * up).astype(BF16)
        return jnp.dot(act, wd_ref[...].astype(BF16), preferred_element_type=F32)

    @pl.when((i == 0) & (j == 0))
    def _():
        h = jnp.concatenate([_rms(x_ref[...], g1_ref[...]).astype(BF16),
                             _rms(xs_ref[...], g1_ref[...]).astype(BF16)], axis=0)
        h_ref[...] = h
        down = swiglu_down(h)
        o_ref[...] = down[:tm]
        os_ref[...] = down[tm:]

    @pl.when((i == 0) & (j > 0))
    def _():
        down = swiglu_down(h_ref[...])
        o_ref[...] += down[:tm]
        os_ref[...] += down[tm:]

    @pl.when((i > 0) & (j == 0))
    def _():
        h = _rms(x_ref[...], g1_ref[...]).astype(BF16)
        h_ref[:tm, :] = h
        o_ref[...] = swiglu_down(h)

    @pl.when((i > 0) & (j > 0))
    def _():
        o_ref[...] += swiglu_down(h_ref[:tm, :])


def _ffn(l, x, xs, g1, wgu, wd):
    m, ns = x.shape[0], xs.shape[0]
    nf = D_FF // TF_FFN
    gain = pl.BlockSpec((None, 1, D_MODEL), lambda i, j: (l, 0, 0))
    return pl.pallas_call(
        _ffn_kernel,
        grid=(m // TM_FFN, nf),
        in_specs=[
            pl.BlockSpec((TM_FFN, D_MODEL), lambda i, j: (i, 0)),
            pl.BlockSpec((ns, D_MODEL), lambda i, j: (0, 0)),
            gain,
            pl.BlockSpec((None, D_MODEL, TF_FFN), lambda i, j: (l, 0, j)),
            pl.BlockSpec((None, D_MODEL, TF_FFN), lambda i, j: (l, 0, nf + j)),
            pl.BlockSpec((None, TF_FFN, D_MODEL), lambda i, j: (l, j, 0)),
        ],
        out_specs=[pl.BlockSpec((TM_FFN, D_MODEL), lambda i, j: (i, 0)),
                   pl.BlockSpec((ns, D_MODEL), lambda i, j: (0, 0))],
        out_shape=[jax.ShapeDtypeStruct((m, D_MODEL), F32), jax.ShapeDtypeStruct((ns, D_MODEL), F32)],
        scratch_shapes=[pltpu.VMEM((TM_FFN + ns, D_MODEL), BF16)],
        compiler_params=_params(("arbitrary", "arbitrary"), vmem_limit=VMEM_LIMIT_FUSED),
        name="ffn",
    )(x, xs, g1, wgu, wgu, wd)


def _ple_kernel(x_ref, f_ref, pe_ref, xs_ref, fs_ref, pes_ref, g_ref, wg_ref, wp_ref, o_ref, os_ref,
                wgres_ref, wpres_ref, x0_ref, pe0_ref, *, nw, nt):
    s = pl.program_id(0)

    def ffn_residual(x1, f):
        return x1 + _rms(f, g_ref[...])

    def tile(x1, f, pe):
        x2 = ffn_residual(x1, f)
        gate = jnp.dot(x2.astype(BF16), wgres_ref[...], preferred_element_type=F32)
        proj = jnp.dot(pe.astype(BF16), wpres_ref[...], preferred_element_type=F32)
        return x2 + jax.nn.sigmoid(gate) * proj

    @pl.when(s == 0)
    def _():
        x2 = ffn_residual(x_ref[...], f_ref[...])
        o_ref[...] = x2
        x0_ref[...] = x2.astype(BF16)
        pe0_ref[...] = pe_ref[...].astype(BF16)

    def tile0(cols, gate_chunk):
        proj_chunk = wp_ref[...].astype(BF16)
        wpres_ref[:, cols] = proj_chunk
        gate = jnp.dot(x0_ref[...], gate_chunk, preferred_element_type=F32)
        proj = jnp.dot(pe0_ref[...], proj_chunk, preferred_element_type=F32)
        o_ref[:, cols] = o_ref[:, cols] + jax.nn.sigmoid(gate) * proj

    _on_weight_chunks(s, wg_ref, wgres_ref, nw, tile0)

    @pl.when(_is_tile_step(s, nw, nt))
    def _():
        o_ref[...] = tile(x_ref[...], f_ref[...], pe_ref[...])

    @pl.when(s == nw + nt - 1)
    def _():
        os_ref[...] = tile(xs_ref[...], fs_ref[...], pes_ref[...])


def _ple(l, x, f, pe, xs, fs, pes, g, wg, wp):
    m, ns = x.shape[0], xs.shape[0]
    nw, nt = D_MODEL // TN_W, m // TM_RES
    tile = lambda s: (_tile_index(s, nw, nt), 0)
    const = lambda s: (0, 0)
    chunk = lambda s: (l, 0, jnp.minimum(s, nw - 1))
    return pl.pallas_call(
        functools.partial(_ple_kernel, nw=nw, nt=nt),
        grid=(nw + nt,),
        in_specs=[
            pl.BlockSpec((TM_RES, D_MODEL), tile),
            pl.BlockSpec((TM_RES, D_MODEL), tile),
            pl.BlockSpec((None, TM_RES, PLE_DIM), lambda s: (l, _tile_index(s, nw, nt), 0)),
            pl.BlockSpec((ns, D_MODEL), const),
            pl.BlockSpec((ns, D_MODEL), const),
            pl.BlockSpec((None, ns, PLE_DIM), lambda s: (l, 0, 0)),
            pl.BlockSpec((None, 1, D_MODEL), lambda s: (l, 0, 0)),
            pl.BlockSpec((None, D_MODEL, TN_W), chunk),
            pl.BlockSpec((None, PLE_DIM, TN_W), chunk),
        ],
        out_specs=[pl.BlockSpec((TM_RES, D_MODEL), tile), pl.BlockSpec((ns, D_MODEL), const)],
        out_shape=[jax.ShapeDtypeStruct((m, D_MODEL), F32), jax.ShapeDtypeStruct((ns, D_MODEL), F32)],
        scratch_shapes=[pltpu.VMEM((D_MODEL, D_MODEL), BF16), pltpu.VMEM((PLE_DIM, D_MODEL), BF16),
                        pltpu.VMEM((TM_RES, D_MODEL), BF16), pltpu.VMEM((TM_RES, PLE_DIM), BF16)],
        compiler_params=_params(("arbitrary",), vmem_limit=VMEM_LIMIT_FUSED),
        name="ple",
    )(x, f, pe, xs, fs, pes, g, wg, wp)


def kernel(x_prompt, x_sample, cache_k, cache_v, state_ssm_re, state_ssm_im, p_prompt, p_sample,
           g_pre_mix, w_in, attn_sinks, ssm_a_re, ssm_a_im, ssm_log_dt, ssm_b_re, ssm_b_im,
           ssm_c_re, ssm_c_im, ssm_d, ssm_w_glu, g_attn_out, g_ssm_out, w_out, g_post_mix,
           g_pre_ffn, w_gate_up, w_down, g_post_ffn, w_ple_gate, w_ple_proj):
    nb, t, _ = x_prompt.shape
    ns = x_sample.shape[0]

    wb, wbd, wcr, wci, wglu, a_dense, a_row = _ssm_prep(ssm_a_re, ssm_a_im, ssm_log_dt, ssm_b_re, ssm_b_im,
                                                        ssm_c_re, ssm_c_im, ssm_w_glu)
    d_row = ssm_d.reshape(DEPTH, N_SLABS, 1, SLAB_CH)
    slope_col = jnp.asarray(_SLOPES, F32).reshape(N_HEADS, 1)
    sink_col = attn_sinks.reshape(DEPTH, N_HEADS, 1)
    gain = lambda g: g.reshape(DEPTH, 1, g.shape[-1])
    g_pre_mix, g_attn_out, g_ssm_out, g_post_mix, g_pre_ffn, g_post_ffn = map(
        gain, (g_pre_mix, g_attn_out, g_ssm_out, g_post_mix, g_pre_ffn, g_post_ffn))

    xp = x_prompt.reshape(nb * t, D_MODEL)
    xs = x_sample.reshape(ns, D_MODEL)
    pp = p_prompt.reshape(DEPTH, nb * t, PLE_DIM)
    ps = p_sample.reshape(DEPTH, ns, PLE_DIM)
    s0r = state_ssm_re.reshape(DEPTH, ns, SSM_GROUPS * SSM_STATE)
    s0i = state_ssm_im.reshape(DEPTH, ns, SSM_GROUPS * SSM_STATE)
    ck = cache_k.reshape(DEPTH, ns, WINDOW, KV_WIDTH)
    cv = cache_v.reshape(DEPTH, ns, WINDOW, KV_WIDTH)

    kv_lo, kv_hi = Q_WIDTH, Q_WIDTH + 2 * KV_WIDTH

    kvp_l, srp_l, sip_l, ks_l, vs_l, srs_l, sis_l = [], [], [], [], [], [], []
    for l in range(DEPTH):
        z, zs, attn = _inproj_attn(l, xp, xs, g_pre_mix, w_in, attn_sinks, t)
        z3 = z.reshape(nb, t, IN_WIDTH)

        ssm, sre, sim = _ssm_prompt(l, z3, wbd, a_dense, wcr, wci, d_row, wglu)

        kvn = zs[:, kv_lo:kv_hi].reshape(ns, 1, 2 * KV_WIDTH)
        attn_s, nks, nvs = _attn_sample(l, zs[:, :Q_WIDTH].reshape(ns, N_HEADS, HEAD_DIM), kvn,
                                        ck, cv, sink_col, slope_col)
        ssm_s, srs, sis = _ssm_sample(l, zs, wb, a_row, s0r, s0i, wcr, wci, d_row, wglu)

        x1, x1s = _outproj(l, attn, ssm.reshape(nb * t, SSM_WIDTH), xp,
                           attn_s.reshape(ns, Q_WIDTH), ssm_s, xs,
                           g_attn_out, g_ssm_out, g_post_mix, w_out)
        f, fs = _ffn(l, x1, x1s, g_pre_ffn, w_gate_up, w_down)
        xp, xs = _ple(l, x1, f, pp, x1s, fs, ps, g_post_ffn, w_ple_gate, w_ple_proj)

        kvp_l.append(z3[:, t - WINDOW:, kv_lo:kv_hi])
        ks_l.append(nks)
        vs_l.append(nvs)
        srp_l.append(sre)
        sip_l.append(sim)
        srs_l.append(srs)
        sis_l.append(sis)

    heads = lambda a: a.reshape(a.shape[:-1] + (N_KV, HEAD_DIM))
    kvp = jnp.stack(kvp_l)
    new_ks = jnp.stack(ks_l)
    new_vs = jnp.stack(vs_l)
    unslab = lambda s: jnp.swapaxes(jnp.stack(s), 1, 2).reshape(DEPTH, nb, SSM_GROUPS, SSM_STATE)
    states = lambda s: jnp.stack(s).reshape(DEPTH, ns, SSM_GROUPS, SSM_STATE)
    return (xp.reshape(nb, t, D_MODEL), xs.reshape(ns, 1, D_MODEL),
            heads(kvp[..., :KV_WIDTH]), heads(kvp[..., KV_WIDTH:]), unslab(srp_l), unslab(sip_l),
            heads(new_ks), heads(new_vs), states(srs_l), states(sis_l))
```
